```python
import jax, jax.numpy as jnp
from jax import lax
import numpy as np

D_MODEL = 2048
BATCH = 16
SEQ = 256
DEPTH = 4
DEC_BATCH = 4
DEC_SEQ = 4096
PAST_LEN = 512

GRID_W = 64
N_MIXERS = 3
N_POOL = (DEPTH + 2) // 3
N_MLA = (DEPTH + 1) // 3
N_CONV = DEPTH // 3
N_SUB = 3
N_MOD = 3 * N_SUB
FFN_HIDDEN = 5632
POOL_WINDOWS = (2, 4, 8, 16)
POOL_GROUP = D_MODEL // len(POOL_WINDOWS)
MLA_HEADS = 16
QK_NOPE = 128
QK_ROPE = 64
V_HEAD = 128
QK_HEAD = QK_NOPE + QK_ROPE
Q_LORA = 512
KV_LORA = 512
ROPE_BASE = 10000.0
CONV_WIDTH = 31
Q_BLOCK = 128
EPS = 1e-6

kernel_name = 'hybrid_pool_mla_conformer_diffusion_step'


def rms_norm(x, g):
    xf = x.astype(jnp.float32)
    y = xf * lax.rsqrt(jnp.mean(xf * xf, axis=-1, keepdims=True) + EPS)
    return (y * g.astype(jnp.float32)).astype(x.dtype)


def layer_norm(x, g, b):
    xf = x.astype(jnp.float32)
    mu = jnp.mean(xf, axis=-1, keepdims=True)
    var = jnp.mean(jnp.square(xf - mu), axis=-1, keepdims=True)
    y = (xf - mu) * lax.rsqrt(var + EPS) * g.astype(jnp.float32) + b.astype(jnp.float32)
    return y.astype(x.dtype)


def modulate(h, shift, scale):
    return h * (1 + scale[:, None, :]) + shift[:, None, :]


def swiglu(h, w1, w3, w2):
    return (jax.nn.silu(h @ w1) * (h @ w3)) @ w2


def pool_mixer(h, w_pool, scale):
    B, T, D = h.shape
    t = jnp.arange(T)
    cs = jnp.concatenate([jnp.zeros((B, 1, D), jnp.float32),
                          jnp.cumsum(h.astype(jnp.float32), axis=1)], axis=1)
    outs = []
    for g, w in enumerate(POOL_WINDOWS):
        lo = jnp.clip(t - w // 2, 0, T - 1)
        hi = jnp.clip(t + w // 2 - 1, 0, T - 1)
        sl = slice(g * POOL_GROUP, (g + 1) * POOL_GROUP)
        csg = cs[:, :, sl]
        cnt = (hi - lo + 1).astype(jnp.float32)[None, :, None]
        mean = (csg[:, hi + 1] - csg[:, lo]) / cnt
        diff = (mean - h[:, :, sl].astype(jnp.float32)).astype(h.dtype)
        outs.append(diff @ w_pool[g])
    return jnp.concatenate(outs, axis=-1) * scale


def conv_module(h, w1, b1, dw, dw_b, ln_g, ln_b, w2, b2):
    u = h @ w1 + b1
    u = u[..., :D_MODEL] * jax.nn.sigmoid(u[..., D_MODEL:])
    u = lax.conv_general_dilated(u, dw[:, None, :], window_strides=(1,),
                                 padding=[(CONV_WIDTH // 2, CONV_WIDTH // 2)],
                                 dimension_numbers=('NWC', 'WIO', 'NWC'),
                                 feature_group_count=D_MODEL) + dw_b
    u = jax.nn.silu(layer_norm(u, ln_g, ln_b))
    return u @ w2 + b2


def axial_rope_tables(T):
    rows = T // GRID_W
    row = jnp.repeat(jnp.arange(rows), GRID_W).astype(jnp.float32)
    col = jnp.tile(jnp.arange(GRID_W), rows).astype(jnp.float32)
    half = QK_ROPE // 2
    inv_freq = ROPE_BASE ** (-jnp.arange(0, half, 2, dtype=jnp.float32) / half)
    ang = jnp.stack([row[:, None] * inv_freq, col[:, None] * inv_freq], axis=1)
    return jnp.cos(ang), jnp.sin(ang)


def rope_part(t, cos, sin):
    nope, rope = t[..., :QK_NOPE], t[..., QK_NOPE:]
    r = rope.astype(jnp.float32).reshape(rope.shape[:-1] + (2, 2, QK_ROPE // 4))
    r1, r2 = r[..., 0, :], r[..., 1, :]
    c = cos[None, :, None]
    s = sin[None, :, None]
    out = jnp.stack([r1 * c - r2 * s, r2 * c + r1 * s], axis=-2).reshape(rope.shape)
    return jnp.concatenate([nope, out.astype(t.dtype)], axis=-1)


def mla_queries(h, p, j):
    B, T, _ = h.shape
    cq = rms_norm(h @ p['mla_w_dq'][j], p['mla_q_norm'][j])
    q = (cq @ p['mla_w_uq'][j]).reshape(B, T, MLA_HEADS, QK_HEAD)
    return rms_norm(q, p['mla_q_gain'][j])


def mla_compress_kv(h, p, j):
    kv = h @ p['mla_w_dkv'][j]
    return rms_norm(kv[..., :KV_LORA], p['mla_kv_norm'][j]), kv[..., KV_LORA:]


def mla_expand_kv(ckv, kpe, p, j):
    B, T, _ = ckv.shape
    kv = (ckv @ p['mla_w_ukv'][j]).reshape(B, T, MLA_HEADS, QK_NOPE + V_HEAD)
    k_nope, v = kv[..., :QK_NOPE], kv[..., QK_NOPE:]
    k_pe = jnp.broadcast_to(kpe[:, :, None, :], (B, T, MLA_HEADS, QK_ROPE))
    k = rms_norm(jnp.concatenate([k_nope, k_pe], axis=-1), p['mla_k_gain'][j])
    return k, v


def block_attention(q, k, v):
    B, Tq, H, dq = q.shape
    nb = Tq // Q_BLOCK
    qb = jnp.moveaxis(q.reshape(B, nb, Q_BLOCK, H, dq), 1, 0)
    scale = QK_HEAD ** -0.5

    def one(qblk):
        s = jnp.einsum('bqhd,bkhd->bhqk', qblk, k).astype(jnp.float32) * scale
        pr = jax.nn.softmax(s, axis=-1).astype(v.dtype)
        return jnp.einsum('bhqk,bkhd->bqhd', pr, v)

    o = lax.map(one, qb)
    return jnp.moveaxis(o, 0, 1).reshape(B, Tq, H, v.shape[-1])


def mla_context(h, p, j):
    B, T, _ = h.shape
    q = mla_queries(h, p, j)
    ckv, kpe = mla_compress_kv(h, p, j)
    k, v = mla_expand_kv(ckv, kpe, p, j)
    o = block_attention(q, k, v).reshape(B, T, MLA_HEADS * V_HEAD)
    return o @ p['mla_w_o'][j], ckv, kpe


def mla_latent(h, ctx_ckv, ctx_kpe, p, j):
    B, T, _ = h.shape
    cos, sin = axial_rope_tables(T)
    q = rope_part(mla_queries(h, p, j), cos, sin)
    ckv, kpe = mla_compress_kv(h, p, j)
    k_lat, v_lat = mla_expand_kv(ckv, kpe, p, j)
    k_lat = rope_part(k_lat, cos, sin)
    k_ctx, v_ctx = mla_expand_kv(ctx_ckv, ctx_kpe, p, j)
    k = jnp.concatenate([k_ctx, k_lat], axis=1)
    v = jnp.concatenate([v_ctx, v_lat], axis=1)
    o = block_attention(q, k, v).reshape(B, T, MLA_HEADS * V_HEAD)
    return o @ p['mla_w_o'][j]


def trunk(x, cond, ckv_cache, kpe_cache, p):
    is_ctx = ckv_cache is None
    new_ckv, new_kpe = [], []
    for i in range(DEPTH):
        mod = jax.nn.silu(cond) @ p['w_ada'][i] + p['b_ada'][i]
        sh1, sc1, g1, sh2, sc2, g2, sh3, sc3, g3 = jnp.split(mod, N_MOD, axis=-1)
        h = modulate(rms_norm(x, p['norm_g'][i, 0]), sh1, sc1)
        x = x + 0.5 * g1[:, None, :] * swiglu(h, p['ffn_w1'][i, 0], p['ffn_w3'][i, 0], p['ffn_w2'][i, 0])
        h = modulate(rms_norm(x, p['norm_g'][i, 1]), sh2, sc2)
        kind, j = i % N_MIXERS, i // N_MIXERS
        if kind == 0:
            y = pool_mixer(h, p['pool_w'][j], p['pool_scale'][j])
        elif kind == 1:
            if is_ctx:
                y, ckv, kpe = mla_context(h, p, j)
                new_ckv.append(ckv)
                new_kpe.append(kpe)
            else:
                y = mla_latent(h, ckv_cache[:, j], kpe_cache[:, j], p, j)
        else:
            y = conv_module(h, p['conv_w1'][j], p['conv_b1'][j], p['conv_dw'][j], p['conv_dw_b'][j],
                            p['conv_ln_g'][j], p['conv_ln_b'][j], p['conv_w2'][j], p['conv_b2'][j])
        x = x + g2[:, None, :] * y
        h = modulate(rms_norm(x, p['norm_g'][i, 2]), sh3, sc3)
        x = x + 0.5 * g3[:, None, :] * swiglu(h, p['ffn_w1'][i, 1], p['ffn_w3'][i, 1], p['ffn_w2'][i, 1])
    return x, new_ckv, new_kpe


def setup_inputs(seed: int = 0) -> dict:
    key = jax.random.key(seed)
    ks = iter(jax.random.split(key, 40))
    D = D_MODEL

    def nrm(shape, scale):
        return jax.random.normal(next(ks), shape, jnp.float32) * scale

    inp = {}
    inp['x_prompt'] = nrm((BATCH, SEQ, D), 1.0)
    inp['x_sample'] = nrm((DEC_BATCH, DEC_SEQ, D), 1.0)
    inp['cache_ckv'] = nrm((DEC_BATCH, N_MLA, PAST_LEN, KV_LORA), 1.0)
    inp['cache_kpe'] = nrm((DEC_BATCH, N_MLA, PAST_LEN, QK_ROPE), 1.0)
    inp['c'] = nrm((DEC_BATCH, D), 1.0)
    inp['c_ctx'] = nrm((D,), 1.0)
    inp['norm_g'] = 1.0 + nrm((DEPTH, N_SUB, D), 0.05)
    inp['w_ada'] = nrm((DEPTH, D, N_MOD * D), 0.5 * D ** -0.5)
    inp['b_ada'] = nrm((DEPTH, N_MOD * D), 0.01)
    inp['ffn_w1'] = nrm((DEPTH, 2, D, FFN_HIDDEN), D ** -0.5)
    inp['ffn_w3'] = nrm((DEPTH, 2, D, FFN_HIDDEN), D ** -0.5)
    inp['ffn_w2'] = nrm((DEPTH, 2, FFN_HIDDEN, D), FFN_HIDDEN ** -0.5)
    inp['pool_w'] = nrm((N_POOL, len(POOL_WINDOWS), POOL_GROUP, POOL_GROUP), POOL_GROUP ** -0.5)
    inp['pool_scale'] = 1.0 + nrm((N_POOL, D), 0.05)
    inp['mla_w_dq'] = nrm((N_MLA, D, Q_LORA), D ** -0.5)
    inp['mla_q_norm'] = 1.0 + nrm((N_MLA, Q_LORA), 0.05)
    inp['mla_w_uq'] = nrm((N_MLA, Q_LORA, MLA_HEADS * QK_HEAD), Q_LORA ** -0.5)
    inp['mla_w_dkv'] = nrm((N_MLA, D, KV_LORA + QK_ROPE), D ** -0.5)
    inp['mla_kv_norm'] = 1.0 + nrm((N_MLA, KV_LORA), 0.05)
    inp['mla_w_ukv'] = nrm((N_MLA, KV_LORA, MLA_HEADS * (QK_NOPE + V_HEAD)), KV_LORA ** -0.5)
    inp['mla_w_o'] = nrm((N_MLA, MLA_HEADS * V_HEAD, D), (MLA_HEADS * V_HEAD) ** -0.5)
    inp['mla_q_gain'] = 1.0 + nrm((N_MLA, QK_HEAD), 0.05)
    inp['mla_k_gain'] = 1.0 + nrm((N_MLA, QK_HEAD), 0.05)
    inp['conv_w1'] = nrm((N_CONV, D, 2 * D), D ** -0.5)
    inp['conv_b1'] = nrm((N_CONV, 2 * D), 0.01)
    inp['conv_dw'] = nrm((N_CONV, CONV_WIDTH, D), CONV_WIDTH ** -0.5)
    inp['conv_dw_b'] = nrm((N_CONV, D), 0.01)
    inp['conv_ln_g'] = 1.0 + nrm((N_CONV, D), 0.05)
    inp['conv_ln_b'] = nrm((N_CONV, D), 0.01)
    inp['conv_w2'] = nrm((N_CONV, D, D), D ** -0.5)
    inp['conv_b2'] = nrm((N_CONV, D), 0.01)
    return inp


def reference(x_prompt, x_sample, cache_ckv, cache_kpe, c, c_ctx, norm_g, w_ada, b_ada,
              ffn_w1, ffn_w3, ffn_w2, pool_w, pool_scale, mla_w_dq, mla_q_norm, mla_w_uq,
              mla_w_dkv, mla_kv_norm, mla_w_ukv, mla_w_o, mla_q_gain, mla_k_gain,
              conv_w1, conv_b1, conv_dw, conv_dw_b, conv_ln_g, conv_ln_b, conv_w2, conv_b2):
    p = dict(norm_g=norm_g, w_ada=w_ada, b_ada=b_ada, ffn_w1=ffn_w1, ffn_w3=ffn_w3, ffn_w2=ffn_w2,
             pool_w=pool_w, pool_scale=pool_scale, mla_w_dq=mla_w_dq, mla_q_norm=mla_q_norm,
             mla_w_uq=mla_w_uq, mla_w_dkv=mla_w_dkv, mla_kv_norm=mla_kv_norm, mla_w_ukv=mla_w_ukv,
             mla_w_o=mla_w_o, mla_q_gain=mla_q_gain, mla_k_gain=mla_k_gain,
             conv_w1=conv_w1, conv_b1=conv_b1, conv_dw=conv_dw, conv_dw_b=conv_dw_b,
             conv_ln_g=conv_ln_g, conv_ln_b=conv_ln_b, conv_w2=conv_w2, conv_b2=conv_b2)
    y_prompt, ckv_list, kpe_list = trunk(x_prompt, c_ctx[None, :], None, None, p)
    state_ckv = jnp.stack(ckv_list, axis=1)
    state_kpe = jnp.stack(kpe_list, axis=1)
    y_sample, _, _ = trunk(x_sample, c, cache_ckv, cache_kpe, p)
    return (y_prompt, y_sample, state_ckv, state_kpe)
```

```python
import functools

import numpy as np
import jax
import jax.numpy as jnp
from jax import lax
from jax.experimental import pallas as pl
from jax.experimental.pallas import tpu as pltpu

F32 = jnp.float32
BF16 = jnp.bfloat16

N_MIXERS = 3
N_MOD = 9
POOL_WINDOWS = (2, 4, 8, 16)
MLA_HEADS = 16
QK_NOPE = 128
QK_ROPE = 64
V_HEAD = 128
QK_HEAD = QK_NOPE + QK_ROPE
GRID_W = 64
ROPE_BASE = 10000.0
EPS = 1e-6

SUBLANES = 8
LANES = 128
VMEM_LIMIT_BYTES = 56 * 1024 * 1024

ROPE_PAD = LANES
HEAD_PAD = ROPE_PAD + QK_NOPE

FFN_TM = 512
FFN_TH = 512
MIX_TM = 256
ADA_TN = 1024
CONV_TN = 1024
ATT_TQ = 256
ATT_TK = 512
POOL_HALO = 8
CONV_HALO = 16
CONV_RB = 64
CONV_CB = 512


def _params(sem):
    return pltpu.CompilerParams(dimension_semantics=sem, vmem_limit_bytes=VMEM_LIMIT_BYTES)


def _norm_mod(x, g, shift, scale):
    y = x * lax.rsqrt(jnp.mean(x * x, axis=-1, keepdims=True) + EPS) * g
    return y * (1.0 + scale) + shift


def _silu(u):
    return u * jax.nn.sigmoid(u)


def _group_index_map(tm, n_prompt, rows_per_group):
    def index_map(i, *_):
        row = i * tm
        grp = jnp.where(row < n_prompt, 0, 1 + (row - n_prompt) // rows_per_group)
        return (grp, 0, 0)
    return index_map


def _ada_kernel(c_ref, w_ref, b_ref, o_ref):
    a = _silu(c_ref[...]).astype(BF16)
    o_ref[...] = jnp.dot(a, w_ref[...].astype(BF16), preferred_element_type=F32) + b_ref[...]


def _ada(cond, w_ada, b_ada):
    depth, d, nout = w_ada.shape
    groups = cond.shape[0]
    g = -(-groups // SUBLANES) * SUBLANES
    cond = jnp.pad(cond, ((0, g - groups), (0, 0)))
    tn = min(ADA_TN, nout)
    out = pl.pallas_call(
        _ada_kernel,
        grid=(depth, nout // tn),
        in_specs=[
            pl.BlockSpec((g, d), lambda l, j: (0, 0)),
            pl.BlockSpec((None, d, tn), lambda l, j: (l, 0, j)),
            pl.BlockSpec((None, 1, tn), lambda l, j: (l, 0, j)),
        ],
        out_specs=pl.BlockSpec((None, g, tn), lambda l, j: (l, 0, j)),
        out_shape=jax.ShapeDtypeStruct((depth, g, nout), F32),
        compiler_params=_params(("arbitrary", "arbitrary")),
        name="ada",
    )(cond, w_ada, b_ada.reshape(depth, 1, nout))
    return out[:, :groups]


def _ffn_kernel(x_ref, mod_ref, g_ref, w1_ref, w3_ref, w2_ref, o_ref, h_ref):
    j = pl.program_id(1)

    @pl.when(j == 0)
    def _():
        h = _norm_mod(x_ref[...], g_ref[...], mod_ref[0:1, :], mod_ref[1:2, :])
        h_ref[...] = h.astype(BF16)

    h = h_ref[...]
    u = jnp.dot(h, w1_ref[...], preferred_element_type=F32)
    v = jnp.dot(h, w3_ref[...], preferred_element_type=F32)
    a = (_silu(u) * v).astype(BF16)
    p = jnp.dot(a, w2_ref[...], preferred_element_type=F32)

    @pl.when(j == 0)
    def _():
        o_ref[...] = p

    @pl.when(j > 0)
    def _():
        o_ref[...] += p

    @pl.when(j == pl.num_programs(1) - 1)
    def _():
        o_ref[...] = x_ref[...] + 0.5 * mod_ref[2:3, :] * o_ref[...]


def _ffn(x, mod, g, w1, w3, w2, n_prompt, rows_per_group):
    n, d = x.shape
    hid = w1.shape[1]
    tm, th = min(FFN_TM, n), min(FFN_TH, hid)
    return pl.pallas_call(
        _ffn_kernel,
        grid=(n // tm, hid // th),
        in_specs=[
            pl.BlockSpec((tm, d), lambda i, j: (i, 0)),
            pl.BlockSpec((None, 3, d), _group_index_map(tm, n_prompt, rows_per_group)),
            pl.BlockSpec((1, d), lambda i, j: (0, 0)),
            pl.BlockSpec((d, th), lambda i, j: (0, j)),
            pl.BlockSpec((d, th), lambda i, j: (0, j)),
            pl.BlockSpec((th, d), lambda i, j: (j, 0)),
        ],
        out_specs=pl.BlockSpec((tm, d), lambda i, j: (i, 0)),
        out_shape=jax.ShapeDtypeStruct((n, d), F32),
        scratch_shapes=[pltpu.VMEM((tm, d), BF16)],
        compiler_params=_params(("arbitrary", "arbitrary")),
        name="ffn",
    )(x, mod, g, w1, w3, w2)


def _tile_positions(tm, n_prompt, seq, n, dec_seq):
    starts = np.arange(0, n, tm)
    is_prompt = starts < n_prompt
    pos0 = np.where(is_prompt, starts % seq, (starts - n_prompt) % dec_seq)
    length = np.where(is_prompt, seq, dec_seq)
    return jnp.asarray(pos0, jnp.int32), jnp.asarray(length, jnp.int32)


def _halo_specs(tm, halo, d, n):
    per = tm // halo
    last = n // halo - 1
    prev = pl.BlockSpec((halo, d), lambda i, *_: (jnp.maximum(i * per - 1, 0), 0))
    nxt = pl.BlockSpec((halo, d), lambda i, *_: (jnp.minimum((i + 1) * per, last), 0))
    return prev, nxt


def _pool_kernel(pos_ref, len_ref, x_ref, xp_ref, xn_ref, mod_ref, g_ref, w_ref, sc_ref, o_ref, e_ref):
    i = pl.program_id(0)
    tm, d = x_ref.shape
    halo = xp_ref.shape[0]
    pos0 = pos_ref[i]
    seqlen = len_ref[i]
    g, shift, scale = g_ref[...], mod_ref[0:1, :], mod_ref[1:2, :]

    has_prev = (pos0 > 0).astype(F32)
    has_next = (pos0 + tm < seqlen).astype(F32)
    e_ref[0:halo, :] = _norm_mod(xp_ref[...], g, shift, scale) * has_prev
    e_ref[halo:halo + tm, :] = _norm_mod(x_ref[...], g, shift, scale)
    e_ref[halo + tm:, :] = _norm_mod(xn_ref[...], g, shift, scale) * has_next

    rows = e_ref.shape[0]
    pos = pos0 + lax.broadcasted_iota(jnp.int32, (tm, 1), 0)
    gw = d // len(POOL_WINDOWS)
    for gi, w in enumerate(POOL_WINDOWS):
        cols = slice(gi * gw, (gi + 1) * gw)
        e = e_ref[:, cols]
        s = pltpu.roll(e, 1, 0) + e
        half = 1
        while 2 * half < w:
            s = pltpu.roll(s, half, 0) + pltpu.roll(s, rows - half, 0)
            half *= 2
        s = s[halo:halo + tm, :]
        hi = jnp.minimum(pos + (w // 2 - 1), seqlen - 1)
        lo = jnp.maximum(pos - w // 2, 0)
        cnt = (hi - lo + 1).astype(F32)
        diff = (s / cnt - e[halo:halo + tm, :]).astype(BF16)
        y = jnp.dot(diff, w_ref[gi], preferred_element_type=F32) * sc_ref[:, cols]
        o_ref[:, cols] = x_ref[:, cols] + mod_ref[2:3, cols] * y


def _pool(x, mod, g, w_pool, pool_scale, pos0, seqlen, n_prompt, rows_per_group):
    n, d = x.shape
    tm = min(MIX_TM, n)
    prev, nxt = _halo_specs(tm, POOL_HALO, d, n)
    ng, gw, _ = w_pool.shape
    grid_spec = pltpu.PrefetchScalarGridSpec(
        num_scalar_prefetch=2,
        grid=(n // tm,),
        in_specs=[
            pl.BlockSpec((tm, d), lambda i, *_: (i, 0)),
            prev, nxt,
            pl.BlockSpec((None, 3, d), _group_index_map(tm, n_prompt, rows_per_group)),
            pl.BlockSpec((1, d), lambda i, *_: (0, 0)),
            pl.BlockSpec((ng, gw, gw), lambda i, *_: (0, 0, 0)),
            pl.BlockSpec((1, d), lambda i, *_: (0, 0)),
        ],
        out_specs=pl.BlockSpec((tm, d), lambda i, *_: (i, 0)),
        scratch_shapes=[pltpu.VMEM((tm + 2 * POOL_HALO, d), F32)],
    )
    return pl.pallas_call(
        _pool_kernel,
        grid_spec=grid_spec,
        out_shape=jax.ShapeDtypeStruct((n, d), F32),
        compiler_params=_params(("arbitrary",)),
        name="pool",
    )(pos0, seqlen, x, x, x, mod, g, w_pool, pool_scale)


def _rotate(main, swapped, cos, sin):
    return main * cos + swapped * sin


def _mla_pre_kernel(x_ref, mod_ref, g_ref, cos_ref, sin_ref, wdq_ref, qn_ref, wq_ref, wqs_ref,
                    wkv_ref, kvn_ref, gq_ref, gqs_ref, q_ref, ckv_ref, kpe_ref, kps_ref):
    h = _norm_mod(x_ref[...], g_ref[...], mod_ref[0:1, :], mod_ref[1:2, :]).astype(BF16)
    cos, sin = cos_ref[...], sin_ref[...]

    cq = jnp.dot(h, wdq_ref[...], preferred_element_type=F32)
    cq = cq * lax.rsqrt(jnp.mean(cq * cq, axis=-1, keepdims=True) + EPS) * qn_ref[...]
    cq = cq.astype(BF16)
    q = jnp.dot(cq, wq_ref[...], preferred_element_type=F32)
    qs = jnp.dot(cq, wqs_ref[...], preferred_element_type=F32)
    gq, gqs = gq_ref[...], gqs_ref[...]
    for hd in range(q_ref.shape[0]):
        qh = q[:, hd * HEAD_PAD:(hd + 1) * HEAD_PAD]
        rinv = lax.rsqrt(jnp.sum(qh * qh, axis=-1, keepdims=True) / QK_HEAD + EPS)
        qh = qh * rinv * gq
        qsw = qs[:, hd * ROPE_PAD:(hd + 1) * ROPE_PAD] * rinv * gqs
        rope = _rotate(qh[:, :ROPE_PAD], qsw, cos, sin)
        q_ref[hd] = jnp.concatenate([rope, qh[:, ROPE_PAD:]], axis=-1).astype(BF16)

    kv = jnp.dot(h, wkv_ref[...], preferred_element_type=F32)
    lora = ckv_ref.shape[1]
    c = kv[:, :lora]
    ckv_ref[...] = c * lax.rsqrt(jnp.mean(c * c, axis=-1, keepdims=True) + EPS) * kvn_ref[...]
    kpe_ref[...] = kv[:, lora:lora + ROPE_PAD]
    kps_ref[...] = kv[:, lora + ROPE_PAD:]


def _mla_pre(x, mod, g, cos, sin, wdq, qn, wq, wqs, wkv, kvn, gq, gqs, n_prompt, rows_per_group):
    n, d = x.shape
    tm = min(MIX_TM, n)
    qlora = wdq.shape[1]
    lora = kvn.shape[1]
    heads = wq.shape[1] // HEAD_PAD
    const = lambda shape: pl.BlockSpec(shape, lambda i: (0,) * len(shape))
    row = lambda width: pl.BlockSpec((tm, width), lambda i: (i, 0))
    return pl.pallas_call(
        _mla_pre_kernel,
        grid=(n // tm,),
        in_specs=[
            row(d),
            pl.BlockSpec((None, 3, d), _group_index_map(tm, n_prompt, rows_per_group)),
            const((1, d)), row(ROPE_PAD), row(ROPE_PAD),
            const((d, qlora)), const((1, qlora)),
            const((qlora, heads * HEAD_PAD)), const((qlora, heads * ROPE_PAD)),
            const((d, lora + 2 * ROPE_PAD)), const((1, lora)),
            const((1, HEAD_PAD)), const((1, ROPE_PAD)),
        ],
        out_specs=[
            pl.BlockSpec((heads, tm, HEAD_PAD), lambda i: (0, i, 0)),
            row(lora), row(ROPE_PAD), row(ROPE_PAD),
        ],
        out_shape=[
            jax.ShapeDtypeStruct((heads, n, HEAD_PAD), BF16),
            jax.ShapeDtypeStruct((n, lora), F32),
            jax.ShapeDtypeStruct((n, ROPE_PAD), F32),
            jax.ShapeDtypeStruct((n, ROPE_PAD), F32),
        ],
        compiler_params=_params(("arbitrary",)),
        name="mla_pre",
    )(x, mod, g, cos, sin, wdq, qn, wq, wqs, wkv, kvn, gq, gqs)


def _mla_expand_kernel(ckv_ref, kpe_ref, kps_ref, cos_ref, sin_ref, w_ref, gk_ref, gks_ref,
                       k_ref, v_ref):
    kv = jnp.dot(ckv_ref[...].astype(BF16), w_ref[...], preferred_element_type=F32)
    gk, gks = gk_ref[...], gks_ref[...]
    kpe, kps = kpe_ref[...], kps_ref[...]
    cos, sin = cos_ref[...], sin_ref[...]
    pe_sq = jnp.sum(kpe * kpe, axis=-1, keepdims=True)
    width = QK_NOPE + V_HEAD
    for hd in range(k_ref.shape[0]):
        kn = kv[:, hd * width:hd * width + QK_NOPE]
        rinv = lax.rsqrt((jnp.sum(kn * kn, axis=-1, keepdims=True) + pe_sq) / QK_HEAD + EPS)
        rope = _rotate(kpe * gk[:, :ROPE_PAD], kps * gks, cos, sin) * rinv
        k_ref[hd] = jnp.concatenate([rope, kn * rinv * gk[:, ROPE_PAD:]], axis=-1).astype(BF16)
        v_ref[hd] = kv[:, hd * width + QK_NOPE:(hd + 1) * width].astype(BF16)


def _mla_expand(ckv, kpe, kps, cos, sin, w_ukv, gk, gks):
    n, lora = ckv.shape
    tm = min(MIX_TM, n)
    heads = w_ukv.shape[1] // (QK_NOPE + V_HEAD)
    const = lambda shape: pl.BlockSpec(shape, lambda i: (0,) * len(shape))
    row = lambda width: pl.BlockSpec((tm, width), lambda i: (i, 0))
    return pl.pallas_call(
        _mla_expand_kernel,
        grid=(n // tm,),
        in_specs=[row(lora), row(ROPE_PAD), row(ROPE_PAD), row(ROPE_PAD), row(ROPE_PAD),
                  const(w_ukv.shape), const((1, HEAD_PAD)), const((1, ROPE_PAD))],
        out_specs=[pl.BlockSpec((heads, tm, HEAD_PAD), lambda i: (0, i, 0)),
                   pl.BlockSpec((heads, tm, V_HEAD), lambda i: (0, i, 0))],
        out_shape=[jax.ShapeDtypeStruct((heads, n, HEAD_PAD), BF16),
                   jax.ShapeDtypeStruct((heads, n, V_HEAD), BF16)],
        compiler_params=_params(("arbitrary",)),
        name="mla_expand",
    )(ckv, kpe, kps, cos, sin, w_ukv, gk, gks)


def _attn_kernel(*refs, n_parts, scale):
    q_ref, o_ref = refs[0], refs[-1]
    q = q_ref[...]
    tq = q.shape[0]
    carry = (jnp.full((tq, 1), -jnp.inf, F32), jnp.zeros((tq, 1), F32),
             jnp.zeros((tq, o_ref.shape[1]), F32))
    for part in range(n_parts):
        k_ref, v_ref = refs[1 + 2 * part], refs[2 + 2 * part]
        tk = min(ATT_TK, k_ref.shape[0])

        def body(c, carry, k_ref=k_ref, v_ref=v_ref, tk=tk):
            m, l, acc = carry
            off = pl.multiple_of(c * tk, tk)
            k = k_ref[pl.ds(off, tk), :]
            v = v_ref[pl.ds(off, tk), :]
            s = lax.dot_general(q, k, (((1,), (1,)), ((), ())), preferred_element_type=F32) * scale
            m_new = jnp.maximum(m, jnp.max(s, axis=-1, keepdims=True))
            alpha = jnp.exp(m - m_new)
            p = jnp.exp(s - m_new)
            l = alpha * l + jnp.sum(p, axis=-1, keepdims=True)
            acc = alpha * acc + jnp.dot(p.astype(BF16), v, preferred_element_type=F32)
            return m_new, l, acc

        carry = lax.fori_loop(0, k_ref.shape[0] // tk, body, carry)
    _, l, acc = carry
    o_ref[...] = (acc / l).astype(BF16)


def _attention(q, parts, q_row0, n_batch, t_q, out_rows):
    heads = q.shape[0]
    tq = min(ATT_TQ, t_q)
    nq = t_q // tq
    qb0 = q_row0 // tq
    in_specs = [pl.BlockSpec((None, tq, HEAD_PAD), lambda b, h, i: (h, qb0 + b * nq + i, 0))]
    args = [q]
    for k, v, row0, t_k in parts:
        kb0 = row0 // t_k
        in_specs.append(pl.BlockSpec((None, t_k, HEAD_PAD), lambda b, h, i, kb0=kb0: (h, kb0 + b, 0)))
        in_specs.append(pl.BlockSpec((None, t_k, V_HEAD), lambda b, h, i, kb0=kb0: (h, kb0 + b, 0)))
        args += [k, v]
    return pl.pallas_call(
        functools.partial(_attn_kernel, n_parts=len(parts), scale=QK_HEAD ** -0.5),
        grid=(n_batch, heads, nq),
        in_specs=in_specs,
        out_specs=pl.BlockSpec((tq, V_HEAD), lambda b, h, i: (b * nq + i, h)),
        out_shape=jax.ShapeDtypeStruct((out_rows, heads * V_HEAD), BF16),
        compiler_params=_params(("arbitrary", "arbitrary", "arbitrary")),
        name="attn",
    )(*args)


def _proj_res_kernel(x_ref, a_ref, mod_ref, w_ref, o_ref):
    y = jnp.dot(a_ref[...], w_ref[...], preferred_element_type=F32)
    o_ref[...] = x_ref[...] + mod_ref[2:3, :] * y


def _proj_res(x, a, mod, w, n_prompt, rows_per_group):
    n, d = x.shape
    tm = min(MIX_TM, n)
    return pl.pallas_call(
        _proj_res_kernel,
        grid=(n // tm,),
        in_specs=[
            pl.BlockSpec((tm, d), lambda i: (i, 0)),
            pl.BlockSpec((tm, a.shape[1]), lambda i: (i, 0)),
            pl.BlockSpec((None, 3, d), _group_index_map(tm, n_prompt, rows_per_group)),
            pl.BlockSpec(w.shape, lambda i: (0, 0)),
        ],
        out_specs=pl.BlockSpec((tm, d), lambda i: (i, 0)),
        out_shape=jax.ShapeDtypeStruct((n, d), F32),
        compiler_params=_params(("arbitrary",)),
        name="proj_res",
    )(x, a, mod, w)


def _glu_kernel(x_ref, mod_ref, g_ref, wa_ref, wb_ref, ba_ref, bb_ref, o_ref, h_ref):
    @pl.when(pl.program_id(1) == 0)
    def _():
        h = _norm_mod(x_ref[...], g_ref[...], mod_ref[0:1, :], mod_ref[1:2, :])
        h_ref[...] = h.astype(BF16)

    h = h_ref[...]
    a = jnp.dot(h, wa_ref[...], preferred_element_type=F32) + ba_ref[...]
    b = jnp.dot(h, wb_ref[...], preferred_element_type=F32) + bb_ref[...]
    o_ref[...] = a * jax.nn.sigmoid(b)


def _glu(x, mod, g, w1, b1, n_prompt, rows_per_group):
    n, d = x.shape
    tm, tn = min(FFN_TM, n), min(CONV_TN, d)
    nj = d // tn
    return pl.pallas_call(
        _glu_kernel,
        grid=(n // tm, nj),
        in_specs=[
            pl.BlockSpec((tm, d), lambda i, j: (i, 0)),
            pl.BlockSpec((None, 3, d), _group_index_map(tm, n_prompt, rows_per_group)),
            pl.BlockSpec((1, d), lambda i, j: (0, 0)),
            pl.BlockSpec((d, tn), lambda i, j: (0, j)),
            pl.BlockSpec((d, tn), lambda i, j: (0, nj + j)),
            pl.BlockSpec((1, tn), lambda i, j: (0, j)),
            pl.BlockSpec((1, tn), lambda i, j: (0, nj + j)),
        ],
        out_specs=pl.BlockSpec((tm, tn), lambda i, j: (i, j)),
        out_shape=jax.ShapeDtypeStruct((n, d), F32),
        scratch_shapes=[pltpu.VMEM((tm, d), BF16)],
        compiler_params=_params(("arbitrary", "arbitrary")),
        name="glu",
    )(x, mod, g, w1, w1, b1, b1)


def _conv_kernel(pos_ref, len_ref, x_ref, u_ref, up_ref, un_ref, mod_ref, dw_ref, dwb_ref,
                 lg_ref, lb_ref, w_ref, b_ref, o_ref, e_ref, c_ref):
    i = pl.program_id(0)
    tm, d = x_ref.shape
    halo = up_ref.shape[0]
    width = dw_ref.shape[0]
    pos0 = pos_ref[i]
    seqlen = len_ref[i]

    has_prev = (pos0 > 0).astype(F32)
    has_next = (pos0 + tm < seqlen).astype(F32)
    e_ref[0:halo, :] = up_ref[...] * has_prev
    e_ref[halo:halo + tm, :] = u_ref[...]
    e_ref[halo + tm:, :] = un_ref[...] * has_next

    rb, cb = min(CONV_RB, tm), min(CONV_CB, d)
    first = halo - width // 2
    for c0 in range(0, d, cb):
        for r0 in range(0, tm, rb):
            acc = jnp.zeros((rb, cb), F32) + dwb_ref[:, c0:c0 + cb]
            for k in range(width):
                row = r0 + first + k
                acc = acc + e_ref[row:row + rb, c0:c0 + cb] * dw_ref[k:k + 1, c0:c0 + cb]
            c_ref[r0:r0 + rb, c0:c0 + cb] = acc

    c = c_ref[...]
    mu = jnp.mean(c, axis=-1, keepdims=True)
    var = jnp.mean(jnp.square(c - mu), axis=-1, keepdims=True)
    y = (c - mu) * lax.rsqrt(var + EPS) * lg_ref[...] + lb_ref[...]
    y = _silu(y).astype(BF16)
    y = jnp.dot(y, w_ref[...], preferred_element_type=F32) + b_ref[...]
    o_ref[...] = x_ref[...] + mod_ref[2:3, :] * y


def _conv(x, u, mod, dw, dw_b, ln_g, ln_b, w2, b2, pos0, seqlen, n_prompt, rows_per_group):
    n, d = x.shape
    tm = min(MIX_TM, n)
    prev, nxt = _halo_specs(tm, CONV_HALO, d, n)
    const = lambda shape: pl.BlockSpec(shape, lambda i, *_: (0,) * len(shape))
    grid_spec = pltpu.PrefetchScalarGridSpec(
        num_scalar_prefetch=2,
        grid=(n // tm,),
        in_specs=[
            pl.BlockSpec((tm, d), lambda i, *_: (i, 0)),
            pl.BlockSpec((tm, d), lambda i, *_: (i, 0)),
            prev, nxt,
            pl.BlockSpec((None, 3, d), _group_index_map(tm, n_prompt, rows_per_group)),
            const(dw.shape), const((1, d)), const((1, d)), const((1, d)),
            const(w2.shape), const((1, d)),
        ],
        out_specs=pl.BlockSpec((tm, d), lambda i, *_: (i, 0)),
        scratch_shapes=[pltpu.VMEM((tm + 2 * CONV_HALO, d), F32), pltpu.VMEM((tm, d), F32)],
    )
    return pl.pallas_call(
        _conv_kernel,
        grid_spec=grid_spec,
        out_shape=jax.ShapeDtypeStruct((n, d), F32),
        compiler_params=_params(("arbitrary",)),
        name="conv",
    )(pos0, seqlen, x, u, u, u, mod, dw, dw_b, ln_g, ln_b, w2, b2)


def _rope_swap(a):
    q = QK_ROPE // 4
    lead = a.shape[:-1]
    return a.reshape(lead + (2, 2, q))[..., ::-1, :].reshape(lead + (QK_ROPE,))


def _pad_rope(a):
    return jnp.pad(a, [(0, 0)] * (a.ndim - 1) + [(0, ROPE_PAD - QK_ROPE)])


def _rope_tables(t):
    rows = t // GRID_W
    row = jnp.repeat(jnp.arange(rows), GRID_W).astype(F32)
    col = jnp.tile(jnp.arange(GRID_W), rows).astype(F32)
    half = QK_ROPE // 2
    inv_freq = ROPE_BASE ** (-jnp.arange(0, half, 2, dtype=F32) / half)
    ang_r, ang_c = row[:, None] * inv_freq, col[:, None] * inv_freq
    cos = jnp.concatenate([jnp.cos(ang_r)] * 2 + [jnp.cos(ang_c)] * 2, axis=-1)
    sin = jnp.concatenate([-jnp.sin(ang_r), jnp.sin(ang_r), -jnp.sin(ang_c), jnp.sin(ang_c)], axis=-1)
    pad = ROPE_PAD - QK_ROPE
    cos = jnp.concatenate([cos, jnp.ones((t, pad), F32)], axis=-1)
    sin = jnp.concatenate([sin, jnp.zeros((t, pad), F32)], axis=-1)
    return cos, sin


def _head_layout(w, heads):
    rows = w.shape[0]
    w = w.reshape(rows, heads, QK_HEAD)
    nope, rope = w[..., :QK_NOPE], w[..., QK_NOPE:]
    main = jnp.concatenate([_pad_rope(rope), nope], axis=-1).reshape(rows, heads * HEAD_PAD)
    swapped = _pad_rope(_rope_swap(rope)).reshape(rows, heads * ROPE_PAD)
    return main, swapped


def _gain_layout(gain):
    nope, rope = gain[:QK_NOPE], gain[QK_NOPE:]
    main = jnp.concatenate([_pad_rope(rope), nope])[None, :]
    return main, _pad_rope(_rope_swap(rope))[None, :]


def kernel(x_prompt, x_sample, cache_ckv, cache_kpe, c, c_ctx, norm_g, w_ada, b_ada, ffn_w1, ffn_w3, ffn_w2, pool_w, pool_scale, mla_w_dq, mla_q_norm, mla_w_uq, mla_w_dkv, mla_kv_norm, mla_w_ukv, mla_w_o, mla_q_gain, mla_k_gain, conv_w1, conv_b1, conv_dw, conv_dw_b, conv_ln_g, conv_ln_b, conv_w2, conv_b2):
    batch, seq, d = x_prompt.shape
    dec_batch, dec_seq, _ = x_sample.shape
    depth = norm_g.shape[0]
    past_len = cache_ckv.shape[2]
    lora = mla_kv_norm.shape[1]
    heads = MLA_HEADS
    n_prompt = batch * seq
    n_sample = dec_batch * dec_seq
    n = n_prompt + n_sample
    rows_per_group = dec_seq
    for tm in (FFN_TM, MIX_TM):
        assert n_prompt % tm == 0 and dec_seq % tm == 0
    assert seq % MIX_TM == 0 and n_prompt % dec_seq == 0 and dec_seq % GRID_W == 0
    state_ckv, state_kpe = [], []

    x = jnp.concatenate([x_prompt.reshape(n_prompt, d), x_sample.reshape(n_sample, d)], axis=0)
    cond = jnp.concatenate([c_ctx[None, :], c], axis=0)
    groups = cond.shape[0]
    mod_all = _ada(cond, w_ada, b_ada).reshape(depth, groups, N_MOD // 3, 3, d)

    pos0, seqlen = _tile_positions(min(MIX_TM, n), n_prompt, seq, n, dec_seq)

    for i in range(depth):
        mods = [mod_all[i, :, s] for s in range(3)]
        ng = lambda s: norm_g[i, s][None, :]
        x = _ffn(x, mods[0], ng(0), ffn_w1[i, 0].astype(BF16), ffn_w3[i, 0].astype(BF16),
                 ffn_w2[i, 0].astype(BF16), n_prompt, rows_per_group)
        kind, j = i % N_MIXERS, i // N_MIXERS
        if kind == 0:
            x = _pool(x, mods[1], ng(1), pool_w[j].astype(BF16), pool_scale[j][None, :], pos0, seqlen,
                      n_prompt, rows_per_group)
        elif kind == 1:
            cos_t, sin_t = _rope_tables(dec_seq)
            cos = jnp.concatenate([jnp.ones((n_prompt, ROPE_PAD), F32)] + [cos_t] * dec_batch, axis=0)
            sin = jnp.concatenate([jnp.zeros((n_prompt, ROPE_PAD), F32)] + [sin_t] * dec_batch, axis=0)
            wq, wqs = _head_layout(mla_w_uq[j], heads)
            wd = mla_w_dkv[j]
            wkv = jnp.concatenate([wd[:, :lora], _pad_rope(wd[:, lora:]),
                                   _pad_rope(_rope_swap(wd[:, lora:]))], axis=-1)
            gq, gqs = _gain_layout(mla_q_gain[j])
            gk, gks = _gain_layout(mla_k_gain[j])
            q, ckv, kpe, kps = _mla_pre(
                x, mods[1], ng(1), cos, sin, mla_w_dq[j].astype(BF16), mla_q_norm[j][None, :],
                wq.astype(BF16), wqs.astype(BF16), wkv.astype(BF16), mla_kv_norm[j][None, :],
                gq, gqs, n_prompt, rows_per_group)
            state_ckv.append(ckv[:n_prompt].reshape(batch, seq, lora))
            state_kpe.append(kpe[:n_prompt, :QK_ROPE].reshape(batch, seq, QK_ROPE))
            w_ukv = mla_w_ukv[j].astype(BF16)
            k_tok, v_tok = _mla_expand(ckv, kpe, kps, cos, sin, w_ukv, gk, gks)
            n_cache = dec_batch * past_len
            k_ctx, v_ctx = _mla_expand(
                cache_ckv[:, j].reshape(n_cache, lora),
                _pad_rope(cache_kpe[:, j].reshape(n_cache, QK_ROPE)),
                jnp.zeros((n_cache, ROPE_PAD), F32),
                jnp.ones((n_cache, ROPE_PAD), F32), jnp.zeros((n_cache, ROPE_PAD), F32),
                w_ukv, gk, gks)
            o_prompt = _attention(q, [(k_tok, v_tok, 0, seq)], 0, batch, seq, n_prompt)
            o_sample = _attention(q, [(k_ctx, v_ctx, 0, past_len), (k_tok, v_tok, n_prompt, dec_seq)],
                                  n_prompt, dec_batch, dec_seq, n_sample)
            o = jnp.concatenate([o_prompt, o_sample], axis=0)
            x = _proj_res(x, o, mods[1], mla_w_o[j].astype(BF16), n_prompt, rows_per_group)
        else:
            u = _glu(x, mods[1], ng(1), conv_w1[j].astype(BF16), conv_b1[j][None, :], n_prompt, rows_per_group)
            x = _conv(x, u, mods[1], conv_dw[j], conv_dw_b[j][None, :], conv_ln_g[j][None, :],
                      conv_ln_b[j][None, :], conv_w2[j].astype(BF16), conv_b2[j][None, :],
                      pos0, seqlen, n_prompt, rows_per_group)
        x = _ffn(x, mods[2], ng(2), ffn_w1[i, 1].astype(BF16), ffn_w3[i, 1].astype(BF16),
                 ffn_w2[i, 1].astype(BF16), n_prompt, rows_per_group)

    y_prompt = x[:n_prompt].reshape(batch, seq, d)
    y_sample = x[n_prompt:].reshape(dec_batch, dec_seq, d)
    return (y_prompt, y_sample, jnp.stack(state_ckv, axis=1), jnp.stack(state_kpe, axis=1))
```

```python
import functools

import numpy as np
import jax
import jax.numpy as jnp
from jax import lax
from jax.experimental import pallas as pl
from jax.experimental.pallas import tpu as pltpu

F32 = jnp.float32
BF16 = jnp.bfloat16

N_MIXERS = 3
N_MOD = 9
POOL_WINDOWS = (2, 4, 8, 16)
MLA_HEADS = 16
QK_NOPE = 128
QK_ROPE = 64
V_HEAD = 128
QK_HEAD = QK_NOPE + QK_ROPE
GRID_W = 64
ROPE_BASE = 10000.0
EPS = 1e-6

SUBLANES = 8
LANES = 128
VMEM_LIMIT_BYTES = 56 * 1024 * 1024

ROPE_PAD = LANES
HEAD_PAD = ROPE_PAD + QK_NOPE

FFN_TM = 512
FFN_TH = 512
MIX_TM = 256
ADA_TN = 1024
CONV_TN = 1024
ATT_TQ = 512
ATT_SUB = 256
POOL_HALO = 8
CONV_HALO = 16
CONV_RB = 64
CONV_CB = 512


def _params(sem):
    return pltpu.CompilerParams(dimension_semantics=sem, vmem_limit_bytes=VMEM_LIMIT_BYTES)


def _norm_mod(x, g, shift, scale):
    y = x * lax.rsqrt(jnp.mean(x * x, axis=-1, keepdims=True) + EPS) * g
    return y * (1.0 + scale) + shift


def _silu(u):
    return u * jax.nn.sigmoid(u)


def _group_index_map(tm, n_prompt, rows_per_group):
    def index_map(i, *_):
        row = i * tm
        grp = jnp.where(row < n_prompt, 0, 1 + (row - n_prompt) // rows_per_group)
        return (grp, 0, 0)
    return index_map


def _ada_kernel(c_ref, w_ref, b_ref, o_ref):
    a = _silu(c_ref[...]).astype(BF16)
    o_ref[...] = jnp.dot(a, w_ref[...].astype(BF16), preferred_element_type=F32) + b_ref[...]


def _ada(cond, w_ada, b_ada):
    depth, d, nout = w_ada.shape
    groups = cond.shape[0]
    g = -(-groups // SUBLANES) * SUBLANES
    cond = jnp.pad(cond, ((0, g - groups), (0, 0)))
    tn = min(ADA_TN, nout)
    out = pl.pallas_call(
        _ada_kernel,
        grid=(depth, nout // tn),
        in_specs=[
            pl.BlockSpec((g, d), lambda l, j: (0, 0)),
            pl.BlockSpec((None, d, tn), lambda l, j: (l, 0, j)),
            pl.BlockSpec((None, 1, tn), lambda l, j: (l, 0, j)),
        ],
        out_specs=pl.BlockSpec((None, g, tn), lambda l, j: (l, 0, j)),
        out_shape=jax.ShapeDtypeStruct((depth, g, nout), F32),
        compiler_params=_params(("arbitrary", "arbitrary")),
        name="ada",
    )(cond, w_ada, b_ada.reshape(depth, 1, nout))
    return out[:, :groups]


def _ffn_kernel(x_ref, mod_ref, g_ref, w1_ref, w3_ref, w2_ref, o_ref, h_ref):
    j = pl.program_id(1)

    @pl.when(j == 0)
    def _():
        x = x_ref[...]
        h = _norm_mod(x, g_ref[...], mod_ref[0:1, :], mod_ref[1:2, :])
        h_ref[...] = h.astype(BF16)
        o_ref[...] = x

    h = h_ref[...]
    u = jnp.dot(h, w1_ref[...], preferred_element_type=F32)
    v = jnp.dot(h, w3_ref[...], preferred_element_type=F32)
    a = (_silu(u) * v).astype(BF16)
    p = jnp.dot(a, w2_ref[...], preferred_element_type=F32)
    o_ref[...] += (0.5 * mod_ref[2:3, :]) * p


def _ffn(x, mod, g, w1, w3, w2, n_prompt, rows_per_group):
    n, d = x.shape
    hid = w1.shape[1]
    tm, th = min(FFN_TM, n), min(FFN_TH, hid)
    return pl.pallas_call(
        _ffn_kernel,
        grid=(n // tm, hid // th),
        in_specs=[
            pl.BlockSpec((tm, d), lambda i, j: (i, 0)),
            pl.BlockSpec((None, 3, d), _group_index_map(tm, n_prompt, rows_per_group)),
            pl.BlockSpec((1, d), lambda i, j: (0, 0)),
            pl.BlockSpec((d, th), lambda i, j: (0, j)),
            pl.BlockSpec((d, th), lambda i, j: (0, j)),
            pl.BlockSpec((th, d), lambda i, j: (j, 0)),
        ],
        out_specs=pl.BlockSpec((tm, d), lambda i, j: (i, 0)),
        out_shape=jax.ShapeDtypeStruct((n, d), F32),
        scratch_shapes=[pltpu.VMEM((tm, d), BF16)],
        compiler_params=_params(("arbitrary", "arbitrary")),
        name="ffn",
    )(x, mod, g, w1, w3, w2)


def _tile_positions(tm, n_prompt, seq, n, dec_seq):
    starts = np.arange(0, n, tm)
    is_prompt = starts < n_prompt
    pos0 = np.where(is_prompt, starts % seq, (starts - n_prompt) % dec_seq)
    length = np.where(is_prompt, seq, dec_seq)
    return jnp.asarray(pos0, jnp.int32), jnp.asarray(length, jnp.int32)


def _halo_specs(tm, halo, d, n):
    per = tm // halo
    last = n // halo - 1
    prev = pl.BlockSpec((halo, d), lambda i, *_: (jnp.maximum(i * per - 1, 0), 0))
    nxt = pl.BlockSpec((halo, d), lambda i, *_: (jnp.minimum((i + 1) * per, last), 0))
    return prev, nxt


def _pool_kernel(pos_ref, len_ref, x_ref, xp_ref, xn_ref, mod_ref, g_ref, w_ref, sc_ref, o_ref, e_ref):
    i = pl.program_id(0)
    tm, d = x_ref.shape
    halo = xp_ref.shape[0]
    pos0 = pos_ref[i]
    seqlen = len_ref[i]
    g, shift, scale = g_ref[...], mod_ref[0:1, :], mod_ref[1:2, :]

    has_prev = (pos0 > 0).astype(F32)
    has_next = (pos0 + tm < seqlen).astype(F32)
    e_ref[0:halo, :] = _norm_mod(xp_ref[...], g, shift, scale) * has_prev
    e_ref[halo:halo + tm, :] = _norm_mod(x_ref[...], g, shift, scale)
    e_ref[halo + tm:, :] = _norm_mod(xn_ref[...], g, shift, scale) * has_next

    rows = e_ref.shape[0]
    pos = pos0 + lax.broadcasted_iota(jnp.int32, (tm, 1), 0)
    gw = d // len(POOL_WINDOWS)
    for gi, w in enumerate(POOL_WINDOWS):
        cols = slice(gi * gw, (gi + 1) * gw)
        e = e_ref[:, cols]
        s = pltpu.roll(e, 1, 0) + e
        half = 1
        while 2 * half < w:
            s = pltpu.roll(s, half, 0) + pltpu.roll(s, rows - half, 0)
            half *= 2
        s = s[halo:halo + tm, :]
        hi = jnp.minimum(pos + (w // 2 - 1), seqlen - 1)
        lo = jnp.maximum(pos - w // 2, 0)
        cnt = (hi - lo + 1).astype(F32)
        diff = (s / cnt - e[halo:halo + tm, :]).astype(BF16)
        y = jnp.dot(diff, w_ref[gi], preferred_element_type=F32) * sc_ref[:, cols]
        o_ref[:, cols] = x_ref[:, cols] + mod_ref[2:3, cols] * y


def _pool(x, mod, g, w_pool, pool_scale, pos0, seqlen, n_prompt, rows_per_group):
    n, d = x.shape
    tm = min(MIX_TM, n)
    prev, nxt = _halo_specs(tm, POOL_HALO, d, n)
    ng, gw, _ = w_pool.shape
    grid_spec = pltpu.PrefetchScalarGridSpec(
        num_scalar_prefetch=2,
        grid=(n // tm,),
        in_specs=[
            pl.BlockSpec((tm, d), lambda i, *_: (i, 0)),
            prev, nxt,
            pl.BlockSpec((None, 3, d), _group_index_map(tm, n_prompt, rows_per_group)),
            pl.BlockSpec((1, d), lambda i, *_: (0, 0)),
            pl.BlockSpec((ng, gw, gw), lambda i, *_: (0, 0, 0)),
            pl.BlockSpec((1, d), lambda i, *_: (0, 0)),
        ],
        out_specs=pl.BlockSpec((tm, d), lambda i, *_: (i, 0)),
        scratch_shapes=[pltpu.VMEM((tm + 2 * POOL_HALO, d), F32)],
    )
    return pl.pallas_call(
        _pool_kernel,
        grid_spec=grid_spec,
        out_shape=jax.ShapeDtypeStruct((n, d), F32),
        compiler_params=_params(("arbitrary",)),
        name="pool",
    )(pos0, seqlen, x, x, x, mod, g, w_pool, pool_scale)


def _rotate(main, swapped, cos, sin):
    return main * cos + swapped * sin


def _mla_pre_kernel(x_ref, mod_ref, g_ref, cos_ref, sin_ref, wdq_ref, qn_ref, wq_ref, wqs_ref,
                    wkv_ref, kvn_ref, gq_ref, gqs_ref, q_ref, ckv_ref, kpe_ref, kps_ref):
    h = _norm_mod(x_ref[...], g_ref[...], mod_ref[0:1, :], mod_ref[1:2, :]).astype(BF16)
    cos, sin = cos_ref[...], sin_ref[...]

    cq = jnp.dot(h, wdq_ref[...], preferred_element_type=F32)
    cq = cq * lax.rsqrt(jnp.mean(cq * cq, axis=-1, keepdims=True) + EPS) * qn_ref[...]
    cq = cq.astype(BF16)
    q = jnp.dot(cq, wq_ref[...], preferred_element_type=F32)
    qs = jnp.dot(cq, wqs_ref[...], preferred_element_type=F32)
    gq, gqs = gq_ref[...], gqs_ref[...]
    for hd in range(q_ref.shape[0]):
        qh = q[:, hd * HEAD_PAD:(hd + 1) * HEAD_PAD]
        rinv = lax.rsqrt(jnp.sum(qh * qh, axis=-1, keepdims=True) / QK_HEAD + EPS)
        qh = qh * rinv * gq
        qsw = qs[:, hd * ROPE_PAD:(hd + 1) * ROPE_PAD] * rinv * gqs
        rope = _rotate(qh[:, :ROPE_PAD], qsw, cos, sin)
        q_ref[hd] = jnp.concatenate([rope, qh[:, ROPE_PAD:]], axis=-1).astype(BF16)

    kv = jnp.dot(h, wkv_ref[...], preferred_element_type=F32)
    lora = ckv_ref.shape[1]
    c = kv[:, :lora]
    ckv_ref[...] = c * lax.rsqrt(jnp.mean(c * c, axis=-1, keepdims=True) + EPS) * kvn_ref[...]
    kpe_ref[...] = kv[:, lora:lora + ROPE_PAD]
    kps_ref[...] = kv[:, lora + ROPE_PAD:]


def _mla_pre(x, mod, g, cos, sin, wdq, qn, wq, wqs, wkv, kvn, gq, gqs, n_prompt, rows_per_group):
    n, d = x.shape
    tm = min(MIX_TM, n)
    qlora = wdq.shape[1]
    lora = kvn.shape[1]
    heads = wq.shape[1] // HEAD_PAD
    const = lambda shape: pl.BlockSpec(shape, lambda i: (0,) * len(shape))
    row = lambda width: pl.BlockSpec((tm, width), lambda i: (i, 0))
    return pl.pallas_call(
        _mla_pre_kernel,
        grid=(n // tm,),
        in_specs=[
            row(d),
            pl.BlockSpec((None, 3, d), _group_index_map(tm, n_prompt, rows_per_group)),
            const((1, d)), row(ROPE_PAD), row(ROPE_PAD),
            const((d, qlora)), const((1, qlora)),
            const((qlora, heads * HEAD_PAD)), const((qlora, heads * ROPE_PAD)),
            const((d, lora + 2 * ROPE_PAD)), const((1, lora)),
            const((1, HEAD_PAD)), const((1, ROPE_PAD)),
        ],
        out_specs=[
            pl.BlockSpec((heads, tm, HEAD_PAD), lambda i: (0, i, 0)),
            row(lora), row(ROPE_PAD), row(ROPE_PAD),
        ],
        out_shape=[
            jax.ShapeDtypeStruct((heads, n, HEAD_PAD), BF16),
            jax.ShapeDtypeStruct((n, lora), F32),
            jax.ShapeDtypeStruct((n, ROPE_PAD), F32),
            jax.ShapeDtypeStruct((n, ROPE_PAD), F32),
        ],
        compiler_params=_params(("arbitrary",)),
        name="mla_pre",
    )(x, mod, g, cos, sin, wdq, qn, wq, wqs, wkv, kvn, gq, gqs)


def _mla_expand_kernel(ckv_ref, kpe_ref, kps_ref, cos_ref, sin_ref, w_ref, gk_ref, gks_ref,
                       k_ref, v_ref):
    kv = jnp.dot(ckv_ref[...].astype(BF16), w_ref[...], preferred_element_type=F32)
    gk, gks = gk_ref[...], gks_ref[...]
    kpe, kps = kpe_ref[...], kps_ref[...]
    cos, sin = cos_ref[...], sin_ref[...]
    pe_sq = jnp.sum(kpe * kpe, axis=-1, keepdims=True)
    width = QK_NOPE + V_HEAD
    for hd in range(k_ref.shape[0]):
        kn = kv[:, hd * width:hd * width + QK_NOPE]
        rinv = lax.rsqrt((jnp.sum(kn * kn, axis=-1, keepdims=True) + pe_sq) / QK_HEAD + EPS)
        rope = _rotate(kpe * gk[:, :ROPE_PAD], kps * gks, cos, sin) * rinv
        k_ref[hd] = jnp.concatenate([rope, kn * rinv * gk[:, ROPE_PAD:]], axis=-1).astype(BF16)
        v_ref[hd] = kv[:, hd * width + QK_NOPE:(hd + 1) * width].astype(BF16)


def _mla_expand(ckv, kpe, kps, cos, sin, w_ukv, gk, gks):
    n, lora = ckv.shape
    tm = min(MIX_TM, n)
    heads = w_ukv.shape[1] // (QK_NOPE + V_HEAD)
    const = lambda shape: pl.BlockSpec(shape, lambda i: (0,) * len(shape))
    row = lambda width: pl.BlockSpec((tm, width), lambda i: (i, 0))
    return pl.pallas_call(
        _mla_expand_kernel,
        grid=(n // tm,),
        in_specs=[row(lora), row(ROPE_PAD), row(ROPE_PAD), row(ROPE_PAD), row(ROPE_PAD),
                  const(w_ukv.shape), const((1, HEAD_PAD)), const((1, ROPE_PAD))],
        out_specs=[pl.BlockSpec((heads, tm, HEAD_PAD), lambda i: (0, i, 0)),
                   pl.BlockSpec((heads, tm, V_HEAD), lambda i: (0, i, 0))],
        out_shape=[jax.ShapeDtypeStruct((heads, n, HEAD_PAD), BF16),
                   jax.ShapeDtypeStruct((heads, n, V_HEAD), BF16)],
        compiler_params=_params(("arbitrary",)),
        name="mla_expand",
    )(ckv, kpe, kps, cos, sin, w_ukv, gk, gks)


def _attn_kernel(*refs, n_parts, scale):
    q_ref, o_ref = refs[0], refs[-1]
    nt = (((1,), (1,)), ((), ()))
    c = scale * np.log2(np.e)
    sub = min(ATT_SUB, q_ref.shape[0])
    for r0 in range(0, q_ref.shape[0], sub):
        q = q_ref[r0:r0 + sub, :]
        s = [lax.dot_general(q, refs[1 + 2 * p][...], nt, preferred_element_type=F32) for p in range(n_parts)]
        m = functools.reduce(jnp.maximum, [jnp.max(sp, axis=-1, keepdims=True) for sp in s])
        l, acc = 0.0, 0.0
        for p in range(n_parts):
            e = jnp.exp2((s[p] - m) * c)
            l = l + jnp.sum(e, axis=-1, keepdims=True)
            acc = acc + jnp.dot(e.astype(BF16), refs[2 + 2 * p][...], preferred_element_type=F32)
        o_ref[r0:r0 + sub, :] = (acc / l).astype(BF16)


def _attention(q, parts, q_row0, n_batch, t_q, out_rows):
    heads = q.shape[0]
    tq = min(ATT_TQ, t_q)
    nq = t_q // tq
    qb0 = q_row0 // tq
    in_specs = [pl.BlockSpec((None, tq, HEAD_PAD), lambda b, h, i: (h, qb0 + b * nq + i, 0))]
    args = [q]
    for k, v, row0, t_k in parts:
        kb0 = row0 // t_k
        in_specs.append(pl.BlockSpec((None, t_k, HEAD_PAD), lambda b, h, i, kb0=kb0: (h, kb0 + b, 0)))
        in_specs.append(pl.BlockSpec((None, t_k, V_HEAD), lambda b, h, i, kb0=kb0: (h, kb0 + b, 0)))
        args += [k, v]
    return pl.pallas_call(
        functools.partial(_attn_kernel, n_parts=len(parts), scale=QK_HEAD ** -0.5),
        grid=(n_batch, heads, nq),
        in_specs=in_specs,
        out_specs=pl.BlockSpec((tq, V_HEAD), lambda b, h, i: (b * nq + i, h)),
        out_shape=jax.ShapeDtypeStruct((out_rows, heads * V_HEAD), BF16),
        compiler_params=_params(("arbitrary", "arbitrary", "arbitrary")),
        name="attn",
    )(*args)


def _proj_res_kernel(x_ref, a_ref, mod_ref, w_ref, o_ref):
    y = jnp.dot(a_ref[...], w_ref[...], preferred_element_type=F32)
    o_ref[...] = x_ref[...] + mod_ref[2:3, :] * y


def _proj_res(x, a, mod, w, n_prompt, rows_per_group):
    n, d = x.shape
    tm = min(MIX_TM, n)
    return pl.pallas_call(
        _proj_res_kernel,
        grid=(n // tm,),
        in_specs=[
            pl.BlockSpec((tm, d), lambda i: (i, 0)),
            pl.BlockSpec((tm, a.shape[1]), lambda i: (i, 0)),
            pl.BlockSpec((None, 3, d), _group_index_map(tm, n_prompt, rows_per_group)),
            pl.BlockSpec(w.shape, lambda i: (0, 0)),
        ],
        out_specs=pl.BlockSpec((tm, d), lambda i: (i, 0)),
        out_shape=jax.ShapeDtypeStruct((n, d), F32),
        compiler_params=_params(("arbitrary",)),
        name="proj_res",
    )(x, a, mod, w)


def _glu_kernel(x_ref, mod_ref, g_ref, wa_ref, wb_ref, ba_ref, bb_ref, o_ref, h_ref):
    @pl.when(pl.program_id(1) == 0)
    def _():
        h = _norm_mod(x_ref[...], g_ref[...], mod_ref[0:1, :], mod_ref[1:2, :])
        h_ref[...] = h.astype(BF16)

    h = h_ref[...]
    a = jnp.dot(h, wa_ref[...], preferred_element_type=F32) + ba_ref[...]
    b = jnp.dot(h, wb_ref[...], preferred_element_type=F32) + bb_ref[...]
    o_ref[...] = a * jax.nn.sigmoid(b)


def _glu(x, mod, g, w1, b1, n_prompt, rows_per_group):
    n, d = x.shape
    tm, tn = min(FFN_TM, n), min(CONV_TN, d)
    nj = d // tn
    return pl.pallas_call(
        _glu_kernel,
        grid=(n // tm, nj),
        in_specs=[
            pl.BlockSpec((tm, d), lambda i, j: (i, 0)),
            pl.BlockSpec((None, 3, d), _group_index_map(tm, n_prompt, rows_per_group)),
            pl.BlockSpec((1, d), lambda i, j: (0, 0)),
            pl.BlockSpec((d, tn), lambda i, j: (0, j)),
            pl.BlockSpec((d, tn), lambda i, j: (0, nj + j)),
            pl.BlockSpec((1, tn), lambda i, j: (0, j)),
            pl.BlockSpec((1, tn), lambda i, j: (0, nj + j)),
        ],
        out_specs=pl.BlockSpec((tm, tn), lambda i, j: (i, j)),
        out_shape=jax.ShapeDtypeStruct((n, d), F32),
        scratch_shapes=[pltpu.VMEM((tm, d), BF16)],
        compiler_params=_params(("arbitrary", "arbitrary")),
        name="glu",
    )(x, mod, g, w1, w1, b1, b1)


def _conv_kernel(pos_ref, len_ref, x_ref, u_ref, up_ref, un_ref, mod_ref, dw_ref, dwb_ref,
                 lg_ref, lb_ref, w_ref, b_ref, o_ref, e_ref, c_ref):
    i = pl.program_id(0)
    tm, d = x_ref.shape
    halo = up_ref.shape[0]
    width = dw_ref.shape[0]
    pos0 = pos_ref[i]
    seqlen = len_ref[i]

    has_prev = (pos0 > 0).astype(F32)
    has_next = (pos0 + tm < seqlen).astype(F32)
    e_ref[0:halo, :] = up_ref[...] * has_prev
    e_ref[halo:halo + tm, :] = u_ref[...]
    e_ref[halo + tm:, :] = un_ref[...] * has_next

    rb, cb = min(CONV_RB, tm), min(CONV_CB, d)
    first = halo - width // 2
    for c0 in range(0, d, cb):
        for r0 in range(0, tm, rb):
            acc = jnp.zeros((rb, cb), F32) + dwb_ref[:, c0:c0 + cb]
            for k in range(width):
                row = r0 + first + k
                acc = acc + e_ref[row:row + rb, c0:c0 + cb] * dw_ref[k:k + 1, c0:c0 + cb]
            c_ref[r0:r0 + rb, c0:c0 + cb] = acc

    c = c_ref[...]
    mu = jnp.mean(c, axis=-1, keepdims=True)
    var = jnp.mean(jnp.square(c - mu), axis=-1, keepdims=True)
    y = (c - mu) * lax.rsqrt(var + EPS) * lg_ref[...] + lb_ref[...]
    y = _silu(y).astype(BF16)
    y = jnp.dot(y, w_ref[...], preferred_element_type=F32) + b_ref[...]
    o_ref[...] = x_ref[...] + mod_ref[2:3, :] * y


def _conv(x, u, mod, dw, dw_b, ln_g, ln_b, w2, b2, pos0, seqlen, n_prompt, rows_per_group):
    n, d = x.shape
    tm = min(MIX_TM, n)
    prev, nxt = _halo_specs(tm, CONV_HALO, d, n)
    const = lambda shape: pl.BlockSpec(shape, lambda i, *_: (0,) * len(shape))
    grid_spec = pltpu.PrefetchScalarGridSpec(
        num_scalar_prefetch=2,
        grid=(n // tm,),
        in_specs=[
            pl.BlockSpec((tm, d), lambda i, *_: (i, 0)),
            pl.BlockSpec((tm, d), lambda i, *_: (i, 0)),
            prev, nxt,
            pl.BlockSpec((None, 3, d), _group_index_map(tm, n_prompt, rows_per_group)),
            const(dw.shape), const((1, d)), const((1, d)), const((1, d)),
            const(w2.shape), const((1, d)),
        ],
        out_specs=pl.BlockSpec((tm, d), lambda i, *_: (i, 0)),
        scratch_shapes=[pltpu.VMEM((tm + 2 * CONV_HALO, d), F32), pltpu.VMEM((tm, d), F32)],
    )
    return pl.pallas_call(
        _conv_kernel,
        grid_spec=grid_spec,
        out_shape=jax.ShapeDtypeStruct((n, d), F32),
        compiler_params=_params(("arbitrary",)),
        name="conv",
    )(pos0, seqlen, x, u, u, u, mod, dw, dw_b, ln_g, ln_b, w2, b2)


def _rope_swap(a):
    q = QK_ROPE // 4
    lead = a.shape[:-1]
    return a.reshape(lead + (2, 2, q))[..., ::-1, :].reshape(lead + (QK_ROPE,))


def _pad_rope(a):
    return jnp.pad(a, [(0, 0)] * (a.ndim - 1) + [(0, ROPE_PAD - QK_ROPE)])


def _rope_tables(t):
    rows = t // GRID_W
    row = jnp.repeat(jnp.arange(rows), GRID_W).astype(F32)
    col = jnp.tile(jnp.arange(GRID_W), rows).astype(F32)
    half = QK_ROPE // 2
    inv_freq = ROPE_BASE ** (-jnp.arange(0, half, 2, dtype=F32) / half)
    ang_r, ang_c = row[:, None] * inv_freq, col[:, None] * inv_freq
    cos = jnp.concatenate([jnp.cos(ang_r)] * 2 + [jnp.cos(ang_c)] * 2, axis=-1)
    sin = jnp.concatenate([-jnp.sin(ang_r), jnp.sin(ang_r), -jnp.sin(ang_c), jnp.sin(ang_c)], axis=-1)
    pad = ROPE_PAD - QK_ROPE
    cos = jnp.concatenate([cos, jnp.ones((t, pad), F32)], axis=-1)
    sin = jnp.concatenate([sin, jnp.zeros((t, pad), F32)], axis=-1)
    return cos, sin


def _head_layout(w, heads):
    rows = w.shape[0]
    w = w.reshape(rows, heads, QK_HEAD)
    nope, rope = w[..., :QK_NOPE], w[..., QK_NOPE:]
    main = jnp.concatenate([_pad_rope(rope), nope], axis=-1).reshape(rows, heads * HEAD_PAD)
    swapped = _pad_rope(_rope_swap(rope)).reshape(rows, heads * ROPE_PAD)
    return main, swapped


def _gain_layout(gain):
    nope, rope = gain[:QK_NOPE], gain[QK_NOPE:]
    main = jnp.concatenate([_pad_rope(rope), nope])[None, :]
    return main, _pad_rope(_rope_swap(rope))[None, :]


def kernel(x_prompt, x_sample, cache_ckv, cache_kpe, c, c_ctx, norm_g, w_ada, b_ada, ffn_w1, ffn_w3, ffn_w2, pool_w, pool_scale, mla_w_dq, mla_q_norm, mla_w_uq, mla_w_dkv, mla_kv_norm, mla_w_ukv, mla_w_o, mla_q_gain, mla_k_gain, conv_w1, conv_b1, conv_dw, conv_dw_b, conv_ln_g, conv_ln_b, conv_w2, conv_b2):
    batch, seq, d = x_prompt.shape
    dec_batch, dec_seq, _ = x_sample.shape
    depth = norm_g.shape[0]
    past_len = cache_ckv.shape[2]
    lora = mla_kv_norm.shape[1]
    heads = MLA_HEADS
    n_prompt = batch * seq
    n_sample = dec_batch * dec_seq
    n = n_prompt + n_sample
    rows_per_group = dec_seq
    for tm in (FFN_TM, MIX_TM):
        assert n_prompt % tm == 0 and dec_seq % tm == 0
    assert seq % MIX_TM == 0 and n_prompt % dec_seq == 0 and dec_seq % GRID_W == 0
    state_ckv, state_kpe = [], []

    x = jnp.concatenate([x_prompt.reshape(n_prompt, d), x_sample.reshape(n_sample, d)], axis=0)
    cond = jnp.concatenate([c_ctx[None, :], c], axis=0)
    groups = cond.shape[0]
    mod_all = _ada(cond, w_ada, b_ada).reshape(depth, groups, N_MOD // 3, 3, d)

    pos0, seqlen = _tile_positions(min(MIX_TM, n), n_prompt, seq, n, dec_seq)

    for i in range(depth):
        mods = [mod_all[i, :, s] for s in range(3)]
        ng = lambda s: norm_g[i, s][None, :]
        x = _ffn(x, mods[0], ng(0), ffn_w1[i, 0].astype(BF16), ffn_w3[i, 0].astype(BF16),
                 ffn_w2[i, 0].astype(BF16), n_prompt, rows_per_group)
        kind, j = i % N_MIXERS, i // N_MIXERS
        if kind == 0:
            x = _pool(x, mods[1], ng(1), pool_w[j].astype(BF16), pool_scale[j][None, :], pos0, seqlen,
                      n_prompt, rows_per_group)
        elif kind == 1:
            cos_t, sin_t = _rope_tables(dec_seq)
            cos = jnp.concatenate([jnp.ones((n_prompt, ROPE_PAD), F32)] + [cos_t] * dec_batch, axis=0)
            sin = jnp.concatenate([jnp.zeros((n_prompt, ROPE_PAD), F32)] + [sin_t] * dec_batch, axis=0)
            wq, wqs = _head_layout(mla_w_uq[j], heads)
            wd = mla_w_dkv[j]
            wkv = jnp.concatenate([wd[:, :lora], _pad_rope(wd[:, lora:]),
                                   _pad_rope(_rope_swap(wd[:, lora:]))], axis=-1)
            gq, gqs = _gain_layout(mla_q_gain[j])
            gk, gks = _gain_layout(mla_k_gain[j])
            q, ckv, kpe, kps = _mla_pre(
                x, mods[1], ng(1), cos, sin, mla_w_dq[j].astype(BF16), mla_q_norm[j][None, :],
                wq.astype(BF16), wqs.astype(BF16), wkv.astype(BF16), mla_kv_norm[j][None, :],
                gq, gqs, n_prompt, rows_per_group)
            state_ckv.append(ckv[:n_prompt].reshape(batch, seq, lora))
            state_kpe.append(kpe[:n_prompt, :QK_ROPE].reshape(batch, seq, QK_ROPE))
            w_ukv = mla_w_ukv[j].astype(BF16)
            k_tok, v_tok = _mla_expand(ckv, kpe, kps, cos, sin, w_ukv, gk, gks)
            n_cache = dec_batch * past_len
            k_ctx, v_ctx = _mla_expand(
                cache_ckv[:, j].reshape(n_cache, lora),
                _pad_rope(cache_kpe[:, j].reshape(n_cache, QK_ROPE)),
                jnp.zeros((n_cache, ROPE_PAD), F32),
                jnp.ones((n_cache, ROPE_PAD), F32), jnp.zeros((n_cache, ROPE_PAD), F32),
                w_ukv, gk, gks)
            o_prompt = _attention(q, [(k_tok, v_tok, 0, seq)], 0, batch, seq, n_prompt)
            o_sample = _attention(q, [(k_ctx, v_ctx, 0, past_len), (k_tok, v_tok, n_prompt, dec_seq)],
                                  n_prompt, dec_batch, dec_seq, n_sample)
            o = jnp.concatenate([o_prompt, o_sample], axis=0)
            x = _proj_res(x, o, mods[1], mla_w_o[j].astype(BF16), n_prompt, rows_per_group)
        else:
            u = _glu(x, mods[1], ng(1), conv_w1[j].astype(BF16), conv_b1[j][None, :], n_prompt, rows_per_group)
            x = _conv(x, u, mods[1], conv_dw[j], conv_dw_b[j][None, :], conv_ln_g[j][None, :],
                      conv_ln_b[j][None, :], conv_w2[j].astype(BF16), conv_b2[j][None, :],
                      pos0, seqlen, n_prompt, rows_per_group)
        x = _ffn(x, mods[2], ng(2), ffn_w1[i, 1].astype(BF16), ffn_w3[i, 1].astype(BF16),
                 ffn_w2[i, 1].astype(BF16), n_prompt, rows_per_group)

    y_prompt = x[:n_prompt].reshape(batch, seq, d)
    y_sample = x[n_prompt:].reshape(dec_batch, dec_seq, d)
    return (y_prompt, y_sample, jnp.stack(state_ckv, axis=1), jnp.stack(state_kpe, axis=1))
```

```python
import functools

import numpy as np
import jax
import jax.numpy as jnp
from jax import lax
from jax.experimental import pallas as pl
from jax.experimental.pallas import tpu as pltpu

F32 = jnp.float32
BF16 = jnp.bfloat16

N_MIXERS = 3
N_MOD = 9
POOL_WINDOWS = (2, 4, 8, 16)
MLA_HEADS = 16
QK_NOPE = 128
QK_ROPE = 64
V_HEAD = 128
QK_HEAD = QK_NOPE + QK_ROPE
GRID_W = 64
ROPE_BASE = 10000.0
EPS = 1e-6

SUBLANES = 8
LANES = 128
VMEM_LIMIT_BYTES = 56 * 1024 * 1024

ROPE_PAD = LANES
HEAD_PAD = ROPE_PAD + QK_NOPE

FFN_TM = 1024
FFN_SUB = 512
FFN_TH = 512
FFN_VMEM_BYTES = 60 * 1024 * 1024
MIX_TM = 256
ADA_TN = 1024
GLU_TM = 512
CONV_TN = 1024
ATT_TQ = 1024
ATT_TK = 2048
ATT_AHEAD = 1
ATT_SUB = 256
POOL_HALO = 8
CONV_HALO = 16
CONV_RB = 64
CONV_CB = 256


def _params(sem, vmem_limit_bytes=VMEM_LIMIT_BYTES):
    return pltpu.CompilerParams(dimension_semantics=sem, vmem_limit_bytes=vmem_limit_bytes)


def _norm_mod(x, g, shift, scale):
    y = x * lax.rsqrt(jnp.mean(x * x, axis=-1, keepdims=True) + EPS) * g
    return y * (1.0 + scale) + shift


def _silu(u):
    return u * jax.nn.sigmoid(u)


def _group_index_map(tm, n_prompt, rows_per_group, row0=0):
    def index_map(i, *_):
        row = row0 + i * tm
        grp = jnp.where(row < n_prompt, 0, 1 + (row - n_prompt) // rows_per_group)
        return (grp, 0, 0)
    return index_map


def _ada_kernel(c_ref, w_ref, b_ref, o_ref):
    a = _silu(c_ref[...]).astype(BF16)
    o_ref[...] = jnp.dot(a, w_ref[...].astype(BF16), preferred_element_type=F32) + b_ref[...]


def _ada(cond, w_ada, b_ada):
    depth, d, nout = w_ada.shape
    groups = cond.shape[0]
    g = -(-groups // SUBLANES) * SUBLANES
    cond = jnp.pad(cond, ((0, g - groups), (0, 0)))
    tn = min(ADA_TN, nout)
    out = pl.pallas_call(
        _ada_kernel,
        grid=(depth, nout // tn),
        in_specs=[
            pl.BlockSpec((g, d), lambda l, j: (0, 0)),
            pl.BlockSpec((None, d, tn), lambda l, j: (l, 0, j)),
            pl.BlockSpec((None, 1, tn), lambda l, j: (l, 0, j)),
        ],
        out_specs=pl.BlockSpec((None, g, tn), lambda l, j: (l, 0, j)),
        out_shape=jax.ShapeDtypeStruct((depth, g, nout), F32),
        compiler_params=_params(("arbitrary", "arbitrary")),
        name="ada",
    )(cond, w_ada, b_ada.reshape(depth, 1, nout))
    return out[:, :groups]


def _ffn_kernel(x_ref, mod_ref, g_ref, w13_ref, w2_ref, o_ref, h_ref):
    j = pl.program_id(1)
    tm = o_ref.shape[0]
    th = w2_ref.shape[0]
    sub = min(FFN_SUB, tm)

    @pl.when(j == 0)
    def _():
        for r0 in range(0, tm, sub):
            x = x_ref[r0:r0 + sub, :]
            h = _norm_mod(x, g_ref[...], mod_ref[0:1, :], mod_ref[1:2, :])
            h_ref[r0:r0 + sub, :] = h.astype(BF16)
            o_ref[r0:r0 + sub, :] = x

    gate = 0.5 * mod_ref[2:3, :]
    for r0 in range(0, tm, sub):
        uv = jnp.dot(h_ref[r0:r0 + sub, :], w13_ref[...], preferred_element_type=F32)
        a = (_silu(uv[:, :th]) * uv[:, th:]).astype(BF16)
        p = jnp.dot(a, w2_ref[...], preferred_element_type=F32)
        o_ref[r0:r0 + sub, :] += gate * p


def _ffn_weights(w1, w3, w2):
    d, hid = w1.shape[-2:]
    th = min(FFN_TH, hid)
    lead = w1.shape[:-2]
    split = lambda w: w.reshape(lead + (d, hid // th, th))
    w13 = jnp.concatenate([split(w1), split(w3)], axis=-1)
    w13 = jnp.moveaxis(w13, -2, -3).astype(BF16)
    return w13, w2.astype(BF16)


def _ffn(x, mod, g, w13, w2, n_prompt, rows_per_group, row0=0, rows=None):
    d = x.shape[1]
    n = x.shape[0] if rows is None else rows
    nj, _, th2 = w13.shape
    th = th2 // 2
    tm = min(FFN_TM, n)
    tile0 = row0 // tm
    return pl.pallas_call(
        _ffn_kernel,
        grid=(n // tm, nj),
        in_specs=[
            pl.BlockSpec((tm, d), lambda i, j: (tile0 + i, 0)),
            pl.BlockSpec((None, 3, d), _group_index_map(tm, n_prompt, rows_per_group, row0)),
            pl.BlockSpec((1, d), lambda i, j: (0, 0)),
            pl.BlockSpec((None, d, th2), lambda i, j: (j, 0, 0)),
            pl.BlockSpec((th, d), lambda i, j: (j, 0)),
        ],
        out_specs=pl.BlockSpec((tm, d), lambda i, j: (i, 0)),
        out_shape=jax.ShapeDtypeStruct((n, d), F32),
        scratch_shapes=[pltpu.VMEM((tm, d), BF16)],
        compiler_params=_params(("arbitrary", "arbitrary"), FFN_VMEM_BYTES),
        name="ffn",
    )(x, mod, g, w13, w2)


def _tile_positions(tm, n_prompt, seq, n, dec_seq):
    starts = np.arange(0, n, tm)
    is_prompt = starts < n_prompt
    pos0 = np.where(is_prompt, starts % seq, (starts - n_prompt) % dec_seq)
    length = np.where(is_prompt, seq, dec_seq)
    return jnp.asarray(pos0, jnp.int32), jnp.asarray(length, jnp.int32)


def _halo_specs(tm, halo, d, n):
    per = tm // halo
    last = n // halo - 1
    prev = pl.BlockSpec((halo, d), lambda i, *_: (jnp.maximum(i * per - 1, 0), 0))
    nxt = pl.BlockSpec((halo, d), lambda i, *_: (jnp.minimum((i + 1) * per, last), 0))
    return prev, nxt


def _pool_kernel(pos_ref, len_ref, x_ref, xp_ref, xn_ref, mod_ref, g_ref, w_ref, sc_ref, o_ref, e_ref):
    i = pl.program_id(0)
    tm, d = x_ref.shape
    halo = xp_ref.shape[0]
    pos0 = pos_ref[i]
    seqlen = len_ref[i]
    g, shift, scale = g_ref[...], mod_ref[0:1, :], mod_ref[1:2, :]

    has_prev = (pos0 > 0).astype(F32)
    has_next = (pos0 + tm < seqlen).astype(F32)
    e_ref[0:halo, :] = _norm_mod(xp_ref[...], g, shift, scale) * has_prev
    e_ref[halo:halo + tm, :] = _norm_mod(x_ref[...], g, shift, scale)
    e_ref[halo + tm:, :] = _norm_mod(xn_ref[...], g, shift, scale) * has_next

    rows = e_ref.shape[0]
    pos = pos0 + lax.broadcasted_iota(jnp.int32, (tm, 1), 0)
    gw = d // len(POOL_WINDOWS)
    for gi, w in enumerate(POOL_WINDOWS):
        cols = slice(gi * gw, (gi + 1) * gw)
        e = e_ref[:, cols]
        s = pltpu.roll(e, 1, 0) + e
        half = 1
        while 2 * half < w:
            s = pltpu.roll(s, half, 0) + pltpu.roll(s, rows - half, 0)
            half *= 2
        s = s[halo:halo + tm, :]
        hi = jnp.minimum(pos + (w // 2 - 1), seqlen - 1)
        lo = jnp.maximum(pos - w // 2, 0)
        cnt = (hi - lo + 1).astype(F32)
        diff = (s / cnt - e[halo:halo + tm, :]).astype(BF16)
        y = jnp.dot(diff, w_ref[gi], preferred_element_type=F32) * sc_ref[:, cols]
        o_ref[:, cols] = x_ref[:, cols] + mod_ref[2:3, cols] * y


def _pool(x, mod, g, w_pool, pool_scale, pos0, seqlen, n_prompt, rows_per_group):
    n, d = x.shape
    tm = min(MIX_TM, n)
    prev, nxt = _halo_specs(tm, POOL_HALO, d, n)
    ng, gw, _ = w_pool.shape
    grid_spec = pltpu.PrefetchScalarGridSpec(
        num_scalar_prefetch=2,
        grid=(n // tm,),
        in_specs=[
            pl.BlockSpec((tm, d), lambda i, *_: (i, 0)),
            prev, nxt,
            pl.BlockSpec((None, 3, d), _group_index_map(tm, n_prompt, rows_per_group)),
            pl.BlockSpec((1, d), lambda i, *_: (0, 0)),
            pl.BlockSpec((ng, gw, gw), lambda i, *_: (0, 0, 0)),
            pl.BlockSpec((1, d), lambda i, *_: (0, 0)),
        ],
        out_specs=pl.BlockSpec((tm, d), lambda i, *_: (i, 0)),
        scratch_shapes=[pltpu.VMEM((tm + 2 * POOL_HALO, d), F32)],
    )
    return pl.pallas_call(
        _pool_kernel,
        grid_spec=grid_spec,
        out_shape=jax.ShapeDtypeStruct((n, d), F32),
        compiler_params=_params(("arbitrary",)),
        name="pool",
    )(pos0, seqlen, x, x, x, mod, g, w_pool, pool_scale)


def _rotate(main, swapped, cos, sin):
    return main * cos + swapped * sin


NT_DIMS = (((1,), (1,)), ((), ()))


def _mla_pre_kernel(x_ref, mod_ref, g_ref, cos_ref, sin_ref, wdq_ref, qn_ref, wq_ref, wqs_ref,
                    wkv_ref, wpet_ref, kvn_ref, gq_ref, gqs_ref, q_ref, ckv_ref, kpe_ref, pet_ref):
    h = _norm_mod(x_ref[...], g_ref[...], mod_ref[0:1, :], mod_ref[1:2, :]).astype(BF16)
    cos, sin = cos_ref[...], sin_ref[...]

    cq = jnp.dot(h, wdq_ref[...], preferred_element_type=F32)
    cq = cq * lax.rsqrt(jnp.mean(cq * cq, axis=-1, keepdims=True) + EPS) * qn_ref[...]
    cq = cq.astype(BF16)
    q = jnp.dot(cq, wq_ref[...], preferred_element_type=F32)
    qs = jnp.dot(cq, wqs_ref[...], preferred_element_type=F32)
    gq, gqs = gq_ref[...], gqs_ref[...]
    for hd in range(q_ref.shape[0]):
        qh = q[:, hd * HEAD_PAD:(hd + 1) * HEAD_PAD]
        rinv = lax.rsqrt(jnp.sum(qh * qh, axis=-1, keepdims=True) / QK_HEAD + EPS)
        qh = qh * rinv * gq
        qsw = qs[:, hd * ROPE_PAD:(hd + 1) * ROPE_PAD] * rinv * gqs
        rope = _rotate(qh[:, :ROPE_PAD], qsw, cos, sin)
        q_ref[hd] = jnp.concatenate([rope, qh[:, ROPE_PAD:]], axis=-1).astype(BF16)

    kv = jnp.dot(h, wkv_ref[...], preferred_element_type=F32)
    lora = ckv_ref.shape[1]
    c = kv[:, :lora]
    ckv_ref[...] = c * lax.rsqrt(jnp.mean(c * c, axis=-1, keepdims=True) + EPS) * kvn_ref[...]
    kpe_ref[...] = kv[:, lora:]
    pet_ref[...] = lax.dot_general(wpet_ref[...], h, NT_DIMS, preferred_element_type=F32)


def _mla_pre(x, mod, g, cos, sin, wdq, qn, wq, wqs, wkv, wpet, kvn, gq, gqs, n_prompt, rows_per_group):
    n, d = x.shape
    tm = min(MIX_TM, n)
    qlora = wdq.shape[1]
    lora = kvn.shape[1]
    heads = wq.shape[1] // HEAD_PAD
    const = lambda shape: pl.BlockSpec(shape, lambda i: (0,) * len(shape))
    row = lambda width: pl.BlockSpec((tm, width), lambda i: (i, 0))
    return pl.pallas_call(
        _mla_pre_kernel,
        grid=(n // tm,),
        in_specs=[
            row(d),
            pl.BlockSpec((None, 3, d), _group_index_map(tm, n_prompt, rows_per_group)),
            const((1, d)), row(ROPE_PAD), row(ROPE_PAD),
            const((d, qlora)), const((1, qlora)),
            const((qlora, heads * HEAD_PAD)), const((qlora, heads * ROPE_PAD)),
            const((d, lora + ROPE_PAD)), const((2 * ROPE_PAD, d)), const((1, lora)),
            const((1, HEAD_PAD)), const((1, ROPE_PAD)),
        ],
        out_specs=[
            pl.BlockSpec((heads, tm, HEAD_PAD), lambda i: (0, i, 0)),
            row(lora), row(ROPE_PAD),
            pl.BlockSpec((2 * ROPE_PAD, tm), lambda i: (0, i)),
        ],
        out_shape=[
            jax.ShapeDtypeStruct((heads, n, HEAD_PAD), BF16),
            jax.ShapeDtypeStruct((n, lora), F32),
            jax.ShapeDtypeStruct((n, ROPE_PAD), F32),
            jax.ShapeDtypeStruct((2 * ROPE_PAD, n), F32),
        ],
        compiler_params=_params(("arbitrary",)),
        name="mla_pre",
    )(x, mod, g, cos, sin, wdq, qn, wq, wqs, wkv, wpet, kvn, gq, gqs)


def _mla_expand_kernel(ckv_ref, pet_ref, cos_ref, sin_ref, wkt_ref, wv_ref, gk_ref, gks_ref,
                       kt_ref, v_ref):
    c = ckv_ref[...].astype(BF16)
    knt = lax.dot_general(wkt_ref[...], c, NT_DIMS, preferred_element_type=F32)
    v = jnp.dot(c, wv_ref[...], preferred_element_type=F32)
    gk, gks = gk_ref[...], gks_ref[...]
    pet, pst = pet_ref[:ROPE_PAD, :], pet_ref[ROPE_PAD:, :]
    rope = _rotate(pet * gk[:ROPE_PAD], pst * gks, cos_ref[...], sin_ref[...])
    pe_sq = jnp.sum(pet * pet, axis=0, keepdims=True)
    for hd in range(kt_ref.shape[0]):
        kn = knt[hd * QK_NOPE:(hd + 1) * QK_NOPE, :]
        rinv = lax.rsqrt((jnp.sum(kn * kn, axis=0, keepdims=True) + pe_sq) / QK_HEAD + EPS)
        kt_ref[hd] = jnp.concatenate([rope * rinv, kn * rinv * gk[ROPE_PAD:]], axis=0).astype(BF16)
        v_ref[hd] = v[:, hd * V_HEAD:(hd + 1) * V_HEAD].astype(BF16)


def _mla_expand(ckv, pet, cos_t, sin_t, wkt, wv, gk, gks):
    n, lora = ckv.shape
    tm = min(MIX_TM, n)
    heads = wv.shape[1] // V_HEAD
    const = lambda shape: pl.BlockSpec(shape, lambda i: (0,) * len(shape))
    col = lambda height: pl.BlockSpec((height, tm), lambda i: (0, i))
    return pl.pallas_call(
        _mla_expand_kernel,
        grid=(n // tm,),
        in_specs=[pl.BlockSpec((tm, lora), lambda i: (i, 0)), col(2 * ROPE_PAD), col(ROPE_PAD), col(ROPE_PAD),
                  const(wkt.shape), const(wv.shape), const((HEAD_PAD, 1)), const((ROPE_PAD, 1))],
        out_specs=[pl.BlockSpec((heads, HEAD_PAD, tm), lambda i: (0, 0, i)),
                   pl.BlockSpec((heads, tm, V_HEAD), lambda i: (0, i, 0))],
        out_shape=[jax.ShapeDtypeStruct((heads, HEAD_PAD, n), BF16),
                   jax.ShapeDtypeStruct((heads, n, V_HEAD), BF16)],
        compiler_params=_params(("arbitrary",)),
        name="mla_expand",
    )(ckv, pet, cos_t, sin_t, wkt, wv, gk, gks)


def _attn_kernel(*refs, n_parts, scale):
    q_ref, o_ref = refs[0], refs[-1]
    c = scale * np.log2(np.e)
    sub = min(ATT_SUB, q_ref.shape[0])
    chunks = []
    for p in range(n_parts):
        kt_ref, v_ref = refs[1 + 2 * p], refs[2 + 2 * p]
        tk = min(ATT_TK, v_ref.shape[0])
        chunks += [(kt_ref, v_ref, k0, tk) for k0 in range(0, v_ref.shape[0], tk)]
    def scores(r0):
        q = q_ref[r0:r0 + sub, :]
        return [jnp.dot(q, kt_ref[:, k0:k0 + tk], preferred_element_type=F32) for kt_ref, _, k0, tk in chunks]

    tq = q_ref.shape[0]
    pending = [scores(r) for r in range(0, min(ATT_AHEAD * sub, tq), sub)]
    for r0 in range(0, tq, sub):
        s = pending.pop(0)
        if r0 + ATT_AHEAD * sub < tq:
            pending.append(scores(r0 + ATT_AHEAD * sub))
        m = functools.reduce(jnp.maximum, [jnp.max(sp, axis=-1, keepdims=True) for sp in s])
        l, acc = 0.0, 0.0
        for sp, (_, v_ref, k0, tk) in zip(s, chunks):
            e = jnp.exp2((sp - m) * c)
            l = l + jnp.sum(e, axis=-1, keepdims=True)
            acc = acc + jnp.dot(e.astype(BF16), v_ref[k0:k0 + tk, :], preferred_element_type=F32)
        o_ref[r0:r0 + sub, :] = (acc / l).astype(BF16)


def _attention(q, parts, q_row0, n_batch, t_q, out_rows):
    heads = q.shape[0]
    tq = min(ATT_TQ, t_q)
    nq = t_q // tq
    qb0 = q_row0 // tq
    in_specs = [pl.BlockSpec((None, tq, HEAD_PAD), lambda b, h, i: (h, qb0 + b * nq + i, 0))]
    args = [q]
    for kt, v, row0, t_k in parts:
        kb0 = row0 // t_k
        in_specs.append(pl.BlockSpec((None, HEAD_PAD, t_k), lambda b, h, i, kb0=kb0: (h, 0, kb0 + b)))
        in_specs.append(pl.BlockSpec((None, t_k, V_HEAD), lambda b, h, i, kb0=kb0: (h, kb0 + b, 0)))
        args += [kt, v]
    return pl.pallas_call(
        functools.partial(_attn_kernel, n_parts=len(parts), scale=QK_HEAD ** -0.5),
        grid=(n_batch, heads, nq),
        in_specs=in_specs,
        out_specs=pl.BlockSpec((tq, V_HEAD), lambda b, h, i: (b * nq + i, h)),
        out_shape=jax.ShapeDtypeStruct((out_rows, heads * V_HEAD), BF16),
        compiler_params=_params(("arbitrary", "arbitrary", "arbitrary")),
        name="attn",
    )(*args)


def _proj_res_kernel(x_ref, a_ref, mod_ref, w_ref, o_ref):
    y = jnp.dot(a_ref[...], w_ref[...], preferred_element_type=F32)
    o_ref[...] = x_ref[...] + mod_ref[2:3, :] * y


def _proj_res(x, a, mod, w, n_prompt, rows_per_group):
    n, d = x.shape
    tm = min(MIX_TM, n)
    return pl.pallas_call(
        _proj_res_kernel,
        grid=(n // tm,),
        in_specs=[
            pl.BlockSpec((tm, d), lambda i: (i, 0)),
            pl.BlockSpec((tm, a.shape[1]), lambda i: (i, 0)),
            pl.BlockSpec((None, 3, d), _group_index_map(tm, n_prompt, rows_per_group)),
            pl.BlockSpec(w.shape, lambda i: (0, 0)),
        ],
        out_specs=pl.BlockSpec((tm, d), lambda i: (i, 0)),
        out_shape=jax.ShapeDtypeStruct((n, d), F32),
        compiler_params=_params(("arbitrary",)),
        name="proj_res",
    )(x, a, mod, w)


def _glu_kernel(x_ref, mod_ref, g_ref, wa_ref, wb_ref, ba_ref, bb_ref, o_ref, h_ref):
    @pl.when(pl.program_id(1) == 0)
    def _():
        h = _norm_mod(x_ref[...], g_ref[...], mod_ref[0:1, :], mod_ref[1:2, :])
        h_ref[...] = h.astype(BF16)

    h = h_ref[...]
    a = jnp.dot(h, wa_ref[...], preferred_element_type=F32) + ba_ref[...]
    b = jnp.dot(h, wb_ref[...], preferred_element_type=F32) + bb_ref[...]
    o_ref[...] = a * jax.nn.sigmoid(b)


def _glu(x, mod, g, w1, b1, n_prompt, rows_per_group):
    n, d = x.shape
    tm, tn = min(GLU_TM, n), min(CONV_TN, d)
    nj = d // tn
    return pl.pallas_call(
        _glu_kernel,
        grid=(n // tm, nj),
        in_specs=[
            pl.BlockSpec((tm, d), lambda i, j: (i, 0)),
            pl.BlockSpec((None, 3, d), _group_index_map(tm, n_prompt, rows_per_group)),
            pl.BlockSpec((1, d), lambda i, j: (0, 0)),
            pl.BlockSpec((d, tn), lambda i, j: (0, j)),
            pl.BlockSpec((d, tn), lambda i, j: (0, nj + j)),
            pl.BlockSpec((1, tn), lambda i, j: (0, j)),
            pl.BlockSpec((1, tn), lambda i, j: (0, nj + j)),
        ],
        out_specs=pl.BlockSpec((tm, tn), lambda i, j: (i, j)),
        out_shape=jax.ShapeDtypeStruct((n, d), F32),
        scratch_shapes=[pltpu.VMEM((tm, d), BF16)],
        compiler_params=_params(("arbitrary", "arbitrary")),
        name="glu",
    )(x, mod, g, w1, w1, b1, b1)


def _conv_kernel(pos_ref, len_ref, x_ref, u_ref, up_ref, un_ref, mod_ref, dw_ref, dwb_ref,
                 lg_ref, lb_ref, w_ref, b_ref, o_ref, e_ref, c_ref):
    i = pl.program_id(0)
    tm, d = x_ref.shape
    halo = up_ref.shape[0]
    width = dw_ref.shape[0]
    pos0 = pos_ref[i]
    seqlen = len_ref[i]

    has_prev = (pos0 > 0).astype(F32)
    has_next = (pos0 + tm < seqlen).astype(F32)
    e_ref[0:halo, :] = up_ref[...] * has_prev
    e_ref[halo:halo + tm, :] = u_ref[...]
    e_ref[halo + tm:, :] = un_ref[...] * has_next

    rb, cb = min(CONV_RB, tm), min(CONV_CB, d)
    first = halo - width // 2
    for c0 in range(0, d, cb):
        cols = slice(c0, c0 + cb)
        for r0 in range(0, tm, rb):
            acc = jnp.zeros((rb, cb), F32) + dwb_ref[:, cols]
            for sp in range(SUBLANES):
                g = None
                for k in range(width):
                    if (first + k) % SUBLANES != sp:
                        continue
                    base = r0 + first + k - sp
                    term = e_ref[base:base + rb + SUBLANES, cols] * dw_ref[k:k + 1, cols]
                    g = term if g is None else g + term
                if g is not None:
                    acc = acc + g[sp:sp + rb, :]
            c_ref[r0:r0 + rb, cols] = acc

    c = c_ref[...]
    mu = jnp.mean(c, axis=-1, keepdims=True)
    var = jnp.mean(jnp.square(c - mu), axis=-1, keepdims=True)
    y = (c - mu) * lax.rsqrt(var + EPS) * lg_ref[...] + lb_ref[...]
    y = _silu(y).astype(BF16)
    y = jnp.dot(y, w_ref[...], preferred_element_type=F32) + b_ref[...]
    o_ref[...] = x_ref[...] + mod_ref[2:3, :] * y


def _conv(x, u, mod, dw, dw_b, ln_g, ln_b, w2, b2, pos0, seqlen, n_prompt, rows_per_group):
    n, d = x.shape
    tm = min(MIX_TM, n)
    prev, nxt = _halo_specs(tm, CONV_HALO, d, n)
    const = lambda shape: pl.BlockSpec(shape, lambda i, *_: (0,) * len(shape))
    grid_spec = pltpu.PrefetchScalarGridSpec(
        num_scalar_prefetch=2,
        grid=(n // tm,),
        in_specs=[
            pl.BlockSpec((tm, d), lambda i, *_: (i, 0)),
            pl.BlockSpec((tm, d), lambda i, *_: (i, 0)),
            prev, nxt,
            pl.BlockSpec((None, 3, d), _group_index_map(tm, n_prompt, rows_per_group)),
            const(dw.shape), const((1, d)), const((1, d)), const((1, d)),
            const(w2.shape), const((1, d)),
        ],
        out_specs=pl.BlockSpec((tm, d), lambda i, *_: (i, 0)),
        scratch_shapes=[pltpu.VMEM((tm + 2 * CONV_HALO, d), F32), pltpu.VMEM((tm, d), F32)],
    )
    return pl.pallas_call(
        _conv_kernel,
        grid_spec=grid_spec,
        out_shape=jax.ShapeDtypeStruct((n, d), F32),
        compiler_params=_params(("arbitrary",)),
        name="conv",
    )(pos0, seqlen, x, u, u, u, mod, dw, dw_b, ln_g, ln_b, w2, b2)


def _rope_swap(a):
    q = QK_ROPE // 4
    lead = a.shape[:-1]
    return a.reshape(lead + (2, 2, q))[..., ::-1, :].reshape(lead + (QK_ROPE,))


def _pad_rope(a):
    return jnp.pad(a, [(0, 0)] * (a.ndim - 1) + [(0, ROPE_PAD - QK_ROPE)])


def _rope_tables(t):
    rows = t // GRID_W
    row = jnp.repeat(jnp.arange(rows), GRID_W).astype(F32)
    col = jnp.tile(jnp.arange(GRID_W), rows).astype(F32)
    half = QK_ROPE // 2
    inv_freq = ROPE_BASE ** (-jnp.arange(0, half, 2, dtype=F32) / half)
    ang_r, ang_c = row[:, None] * inv_freq, col[:, None] * inv_freq
    cos = jnp.concatenate([jnp.cos(ang_r)] * 2 + [jnp.cos(ang_c)] * 2, axis=-1)
    sin = jnp.concatenate([-jnp.sin(ang_r), jnp.sin(ang_r), -jnp.sin(ang_c), jnp.sin(ang_c)], axis=-1)
    pad = ROPE_PAD - QK_ROPE
    cos = jnp.concatenate([cos, jnp.ones((t, pad), F32)], axis=-1)
    sin = jnp.concatenate([sin, jnp.zeros((t, pad), F32)], axis=-1)
    return cos, sin


def _head_layout(w, heads):
    rows = w.shape[0]
    w = w.reshape(rows, heads, QK_HEAD)
    nope, rope = w[..., :QK_NOPE], w[..., QK_NOPE:]
    main = jnp.concatenate([_pad_rope(rope), nope], axis=-1).reshape(rows, heads * HEAD_PAD)
    swapped = _pad_rope(_rope_swap(rope)).reshape(rows, heads * ROPE_PAD)
    return main, swapped


def _gain_layout(gain):
    nope, rope = gain[:QK_NOPE], gain[QK_NOPE:]
    main = jnp.concatenate([_pad_rope(rope), nope])[None, :]
    return main, _pad_rope(_rope_swap(rope))[None, :]


def kernel(x_prompt, x_sample, cache_ckv, cache_kpe, c, c_ctx, norm_g, w_ada, b_ada, ffn_w1, ffn_w3, ffn_w2, pool_w, pool_scale, mla_w_dq, mla_q_norm, mla_w_uq, mla_w_dkv, mla_kv_norm, mla_w_ukv, mla_w_o, mla_q_gain, mla_k_gain, conv_w1, conv_b1, conv_dw, conv_dw_b, conv_ln_g, conv_ln_b, conv_w2, conv_b2):
    batch, seq, d = x_prompt.shape
    dec_batch, dec_seq, _ = x_sample.shape
    depth = norm_g.shape[0]
    past_len = cache_ckv.shape[2]
    lora = mla_kv_norm.shape[1]
    heads = MLA_HEADS
    n_prompt = batch * seq
    n_sample = dec_batch * dec_seq
    n = n_prompt + n_sample
    rows_per_group = dec_seq
    for tm in (FFN_TM, MIX_TM):
        assert n_prompt % tm == 0 and dec_seq % tm == 0
    assert seq % MIX_TM == 0 and n_prompt % dec_seq == 0 and dec_seq % GRID_W == 0
    state_ckv, state_kpe = [], []

    x = jnp.concatenate([x_prompt.reshape(n_prompt, d), x_sample.reshape(n_sample, d)], axis=0)
    cond = jnp.concatenate([c_ctx[None, :], c], axis=0)
    groups = cond.shape[0]
    mod_all = _ada(cond, w_ada, b_ada).reshape(depth, groups, N_MOD // 3, 3, d)

    pos0, seqlen = _tile_positions(min(MIX_TM, n), n_prompt, seq, n, dec_seq)

    w13, w2 = _ffn_weights(ffn_w1, ffn_w3, ffn_w2)

    for i in range(depth):
        mods = [mod_all[i, :, s] for s in range(3)]
        ng = lambda s: norm_g[i, s][None, :]
        x = _ffn(x, mods[0], ng(0), w13[i, 0], w2[i, 0], n_prompt, rows_per_group)
        kind, j = i % N_MIXERS, i // N_MIXERS
        if kind == 0:
            x = _pool(x, mods[1], ng(1), pool_w[j].astype(BF16), pool_scale[j][None, :], pos0, seqlen,
                      n_prompt, rows_per_group)
        elif kind == 1:
            cos_t, sin_t = _rope_tables(dec_seq)
            cos = jnp.concatenate([jnp.ones((n_prompt, ROPE_PAD), F32)] + [cos_t] * dec_batch, axis=0)
            sin = jnp.concatenate([jnp.zeros((n_prompt, ROPE_PAD), F32)] + [sin_t] * dec_batch, axis=0)
            wq, wqs = _head_layout(mla_w_uq[j], heads)
            wd = mla_w_dkv[j]
            wpe = _pad_rope(wd[:, lora:])
            wkv = jnp.concatenate([wd[:, :lora], wpe], axis=-1)
            wpet = jnp.concatenate([wpe, _pad_rope(_rope_swap(wd[:, lora:]))], axis=-1).T
            gq, gqs = _gain_layout(mla_q_gain[j])
            gk, gks = _gain_layout(mla_k_gain[j])
            q, ckv, kpe, pet = _mla_pre(
                x, mods[1], ng(1), cos, sin, mla_w_dq[j].astype(BF16), mla_q_norm[j][None, :],
                wq.astype(BF16), wqs.astype(BF16), wkv.astype(BF16), wpet.astype(BF16),
                mla_kv_norm[j][None, :], gq, gqs, n_prompt, rows_per_group)
            state_ckv.append(ckv[:n_prompt].reshape(batch, seq, lora))
            state_kpe.append(kpe[:n_prompt, :QK_ROPE].reshape(batch, seq, QK_ROPE))
            w_ukv = mla_w_ukv[j].reshape(lora, heads, QK_NOPE + V_HEAD)
            wkt = w_ukv[..., :QK_NOPE].reshape(lora, heads * QK_NOPE).T.astype(BF16)
            wv = w_ukv[..., QK_NOPE:].reshape(lora, heads * V_HEAD).astype(BF16)
            k_tok, v_tok = _mla_expand(ckv, pet, cos.T, sin.T, wkt, wv, gk.T, gks.T)
            n_cache = dec_batch * past_len
            pet_ctx = jnp.concatenate([_pad_rope(cache_kpe[:, j].reshape(n_cache, QK_ROPE)).T,
                                       jnp.zeros((ROPE_PAD, n_cache), F32)], axis=0)
            k_ctx, v_ctx = _mla_expand(
                cache_ckv[:, j].reshape(n_cache, lora), pet_ctx,
                jnp.ones((ROPE_PAD, n_cache), F32), jnp.zeros((ROPE_PAD, n_cache), F32),
                wkt, wv, gk.T, gks.T)
            o_prompt = _attention(q, [(k_tok, v_tok, 0, seq)], 0, batch, seq, n_prompt)
            o_sample = _attention(q, [(k_ctx, v_ctx, 0, past_len), (k_tok, v_tok, n_prompt, dec_seq)],
                                  n_prompt, dec_batch, dec_seq, n_sample)
            o = jnp.concatenate([o_prompt, o_sample], axis=0)
            x = _proj_res(x, o, mods[1], mla_w_o[j].astype(BF16), n_prompt, rows_per_group)
        else:
            u = _glu(x, mods[1], ng(1), conv_w1[j].astype(BF16), conv_b1[j][None, :], n_prompt, rows_per_group)
            x = _conv(x, u, mods[1], conv_dw[j], conv_dw_b[j][None, :], conv_ln_g[j][None, :],
                      conv_ln_b[j][None, :], conv_w2[j].astype(BF16), conv_b2[j][None, :],
                      pos0, seqlen, n_prompt, rows_per_group)
        if i + 1 < depth:
            x = _ffn(x, mods[2], ng(2), w13[i, 1], w2[i, 1], n_prompt, rows_per_group)
        else:
            last = (mods[2], ng(2), w13[i, 1], w2[i, 1], n_prompt, rows_per_group)
            y_prompt = _ffn(x, *last, row0=0, rows=n_prompt).reshape(batch, seq, d)
            y_sample = _ffn(x, *last, row0=n_prompt, rows=n_sample).reshape(dec_batch, dec_seq, d)

    return (y_prompt, y_sample, jnp.stack(state_ckv, axis=1), jnp.stack(state_kpe, axis=1))
```

```python
import functools

import numpy as np
import jax
import jax.numpy as jnp
from jax import lax
from jax.experimental import pallas as pl
from jax.experimental.pallas import tpu as pltpu

F32 = jnp.float32
BF16 = jnp.bfloat16

N_MIXERS = 3
N_MOD = 9
POOL_WINDOWS = (2, 4, 8, 16)
MLA_HEADS = 16
QK_NOPE = 128
QK_ROPE = 64
V_HEAD = 128
QK_HEAD = QK_NOPE + QK_ROPE
GRID_W = 64
ROPE_BASE = 10000.0
EPS = 1e-6

SUBLANES = 8
LANES = 128
VMEM_LIMIT_BYTES = 56 * 1024 * 1024

ROPE_PAD = LANES
HEAD_PAD = ROPE_PAD + QK_NOPE

FFN_TM = 1024
FFN_SUB = 512
FFN_SUB0 = 256
FFN_TH = 512
FFN_VMEM_BYTES = 60 * 1024 * 1024
MIX_TM = 256
ADA_TN = 1024
GLU_TM = 512
CONV_TN = 1024
ATT_TQ = 1024
ATT_TK = 2048
ATT_AHEAD = 1
ATT_SUB = 256
POOL_HALO = 8
CONV_HALO = 16
CONV_RB = 64
CONV_CB = 256


def _params(sem, vmem_limit_bytes=VMEM_LIMIT_BYTES):
    return pltpu.CompilerParams(dimension_semantics=sem, vmem_limit_bytes=vmem_limit_bytes)


def _norm_mod(x, g, shift, scale):
    y = x * lax.rsqrt(jnp.mean(x * x, axis=-1, keepdims=True) + EPS) * g
    return y * (1.0 + scale) + shift


def _silu(u):
    return u * jax.nn.sigmoid(u)


def _group_index_map(tm, n_prompt, rows_per_group, row0=0):
    def index_map(i, *_):
        row = row0 + i * tm
        grp = jnp.where(row < n_prompt, 0, 1 + (row - n_prompt) // rows_per_group)
        return (grp, 0, 0)
    return index_map


def _ada_kernel(c_ref, w_ref, b_ref, o_ref):
    a = _silu(c_ref[...]).astype(BF16)
    o_ref[...] = jnp.dot(a, w_ref[...].astype(BF16), preferred_element_type=F32) + b_ref[...]


def _ada(cond, w_ada, b_ada):
    depth, d, nout = w_ada.shape
    groups = cond.shape[0]
    g = -(-groups // SUBLANES) * SUBLANES
    cond = jnp.pad(cond, ((0, g - groups), (0, 0)))
    tn = min(ADA_TN, nout)
    out = pl.pallas_call(
        _ada_kernel,
        grid=(depth, nout // tn),
        in_specs=[
            pl.BlockSpec((g, d), lambda l, j: (0, 0)),
            pl.BlockSpec((None, d, tn), lambda l, j: (l, 0, j)),
            pl.BlockSpec((None, 1, tn), lambda l, j: (l, 0, j)),
        ],
        out_specs=pl.BlockSpec((None, g, tn), lambda l, j: (l, 0, j)),
        out_shape=jax.ShapeDtypeStruct((depth, g, nout), F32),
        compiler_params=_params(("arbitrary", "arbitrary")),
        name="ada",
    )(cond, w_ada, b_ada.reshape(depth, 1, nout))
    return out[:, :groups]


def _ffn_kernel(x_ref, mod_ref, g_ref, w13_ref, w2_ref, o_ref, h_ref):
    j = pl.program_id(1)
    tm = o_ref.shape[0]
    th = w2_ref.shape[0]
    gate = 0.5 * mod_ref[2:3, :]

    def gated_swiglu(h):
        uv = jnp.dot(h, w13_ref[...], preferred_element_type=F32)
        a = (_silu(uv[:, :th]) * uv[:, th:]).astype(BF16)
        return gate * jnp.dot(a, w2_ref[...], preferred_element_type=F32)

    @pl.when(j == 0)
    def _():
        sub = min(FFN_SUB0, tm)
        gs = g_ref[...] * (1.0 + mod_ref[1:2, :])
        shift = mod_ref[0:1, :]

        def norm(r0):
            x = x_ref[r0:r0 + sub, :]
            h = x * lax.rsqrt(jnp.mean(x * x, axis=-1, keepdims=True) + EPS) * gs + shift
            h = h.astype(BF16)
            h_ref[r0:r0 + sub, :] = h
            return x, h

        pending = norm(0)
        for r0 in range(0, tm, sub):
            x, h = pending
            if r0 + sub < tm:
                pending = norm(r0 + sub)
            o_ref[r0:r0 + sub, :] = x + gated_swiglu(h)

    @pl.when(j > 0)
    def _():
        sub = min(FFN_SUB, tm)
        for r0 in range(0, tm, sub):
            o_ref[r0:r0 + sub, :] += gated_swiglu(h_ref[r0:r0 + sub, :])


def _ffn_prep_kernel(w1_ref, w3_ref, w2_ref, o13_ref, o2_ref):
    th = w1_ref.shape[1]
    o13_ref[:, :th] = w1_ref[...].astype(BF16)
    o13_ref[:, th:] = w3_ref[...].astype(BF16)
    o2_ref[...] = w2_ref[...].astype(BF16)


def _ffn_weights(w1, w3, w2):
    depth, halves, d, hid = w1.shape
    th = min(FFN_TH, hid)
    nj = hid // th
    return pl.pallas_call(
        _ffn_prep_kernel,
        grid=(depth, halves, nj),
        in_specs=[
            pl.BlockSpec((None, None, d, th), lambda l, s, j: (l, s, 0, j)),
            pl.BlockSpec((None, None, d, th), lambda l, s, j: (l, s, 0, j)),
            pl.BlockSpec((None, None, th, d), lambda l, s, j: (l, s, j, 0)),
        ],
        out_specs=[
            pl.BlockSpec((None, None, None, d, 2 * th), lambda l, s, j: (l, s, j, 0, 0)),
            pl.BlockSpec((None, None, th, d), lambda l, s, j: (l, s, j, 0)),
        ],
        out_shape=[
            jax.ShapeDtypeStruct((depth, halves, nj, d, 2 * th), BF16),
            jax.ShapeDtypeStruct((depth, halves, hid, d), BF16),
        ],
        compiler_params=_params(("arbitrary", "arbitrary", "arbitrary")),
        name="ffn_prep",
    )(w1, w3, w2)


def _ffn(x, mod, g, w13, w2, layer, half, n_prompt, rows_per_group, row0=0, rows=None):
    d = x.shape[1]
    n = x.shape[0] if rows is None else rows
    nj, _, th2 = w13.shape[2:]
    th = th2 // 2
    tm = min(FFN_TM, n)
    tile0 = row0 // tm
    return pl.pallas_call(
        _ffn_kernel,
        grid=(n // tm, nj),
        in_specs=[
            pl.BlockSpec((tm, d), lambda i, j: (tile0 + i, 0)),
            pl.BlockSpec((None, 3, d), _group_index_map(tm, n_prompt, rows_per_group, row0)),
            pl.BlockSpec((1, d), lambda i, j: (0, 0)),
            pl.BlockSpec((None, None, None, d, th2), lambda i, j: (layer, half, j, 0, 0)),
            pl.BlockSpec((None, None, th, d), lambda i, j: (layer, half, j, 0)),
        ],
        out_specs=pl.BlockSpec((tm, d), lambda i, j: (i, 0)),
        out_shape=jax.ShapeDtypeStruct((n, d), F32),
        scratch_shapes=[pltpu.VMEM((tm, d), BF16)],
        compiler_params=_params(("arbitrary", "arbitrary"), FFN_VMEM_BYTES),
        name="ffn",
    )(x, mod, g, w13, w2)


def _tile_positions(tm, n_prompt, seq, n, dec_seq):
    starts = np.arange(0, n, tm)
    is_prompt = starts < n_prompt
    pos0 = np.where(is_prompt, starts % seq, (starts - n_prompt) % dec_seq)
    length = np.where(is_prompt, seq, dec_seq)
    return jnp.asarray(pos0, jnp.int32), jnp.asarray(length, jnp.int32)


def _halo_specs(tm, halo, d, n):
    per = tm // halo
    last = n // halo - 1
    prev = pl.BlockSpec((halo, d), lambda i, *_: (jnp.maximum(i * per - 1, 0), 0))
    nxt = pl.BlockSpec((halo, d), lambda i, *_: (jnp.minimum((i + 1) * per, last), 0))
    return prev, nxt


def _pool_kernel(pos_ref, len_ref, x_ref, xp_ref, xn_ref, mod_ref, g_ref, w_ref, sc_ref, o_ref, e_ref):
    i = pl.program_id(0)
    tm, d = x_ref.shape
    halo = xp_ref.shape[0]
    pos0 = pos_ref[i]
    seqlen = len_ref[i]
    g, shift, scale = g_ref[...], mod_ref[0:1, :], mod_ref[1:2, :]

    has_prev = (pos0 > 0).astype(F32)
    has_next = (pos0 + tm < seqlen).astype(F32)
    e_ref[0:halo, :] = _norm_mod(xp_ref[...], g, shift, scale) * has_prev
    e_ref[halo:halo + tm, :] = _norm_mod(x_ref[...], g, shift, scale)
    e_ref[halo + tm:, :] = _norm_mod(xn_ref[...], g, shift, scale) * has_next

    rows = e_ref.shape[0]
    pos = pos0 + lax.broadcasted_iota(jnp.int32, (tm, 1), 0)
    gw = d // len(POOL_WINDOWS)
    for gi, w in enumerate(POOL_WINDOWS):
        cols = slice(gi * gw, (gi + 1) * gw)
        e = e_ref[:, cols]
        s = pltpu.roll(e, 1, 0) + e
        half = 1
        while 2 * half < w:
            s = pltpu.roll(s, half, 0) + pltpu.roll(s, rows - half, 0)
            half *= 2
        s = s[halo:halo + tm, :]
        hi = jnp.minimum(pos + (w // 2 - 1), seqlen - 1)
        lo = jnp.maximum(pos - w // 2, 0)
        cnt = (hi - lo + 1).astype(F32)
        diff = (s / cnt - e[halo:halo + tm, :]).astype(BF16)
        y = jnp.dot(diff, w_ref[gi], preferred_element_type=F32) * sc_ref[:, cols]
        o_ref[:, cols] = x_ref[:, cols] + mod_ref[2:3, cols] * y


def _pool(x, mod, g, w_pool, pool_scale, pos0, seqlen, n_prompt, rows_per_group):
    n, d = x.shape
    tm = min(MIX_TM, n)
    prev, nxt = _halo_specs(tm, POOL_HALO, d, n)
    ng, gw, _ = w_pool.shape
    grid_spec = pltpu.PrefetchScalarGridSpec(
        num_scalar_prefetch=2,
        grid=(n // tm,),
        in_specs=[
            pl.BlockSpec((tm, d), lambda i, *_: (i, 0)),
            prev, nxt,
            pl.BlockSpec((None, 3, d), _group_index_map(tm, n_prompt, rows_per_group)),
            pl.BlockSpec((1, d), lambda i, *_: (0, 0)),
            pl.BlockSpec((ng, gw, gw), lambda i, *_: (0, 0, 0)),
            pl.BlockSpec((1, d), lambda i, *_: (0, 0)),
        ],
        out_specs=pl.BlockSpec((tm, d), lambda i, *_: (i, 0)),
        scratch_shapes=[pltpu.VMEM((tm + 2 * POOL_HALO, d), F32)],
    )
    return pl.pallas_call(
        _pool_kernel,
        grid_spec=grid_spec,
        out_shape=jax.ShapeDtypeStruct((n, d), F32),
        compiler_params=_params(("arbitrary",)),
        name="pool",
    )(pos0, seqlen, x, x, x, mod, g, w_pool, pool_scale)


def _rotate(main, swapped, cos, sin):
    return main * cos + swapped * sin


NT_DIMS = (((1,), (1,)), ((), ()))


def _mla_pre_kernel(x_ref, mod_ref, g_ref, cos_ref, sin_ref, wdq_ref, qn_ref, wq_ref, wqs_ref,
                    wkv_ref, wpet_ref, kvn_ref, gq_ref, gqs_ref, q_ref, ckv_ref, kpe_ref, pet_ref):
    h = _norm_mod(x_ref[...], g_ref[...], mod_ref[0:1, :], mod_ref[1:2, :]).astype(BF16)
    cos, sin = cos_ref[...], sin_ref[...]

    cq = jnp.dot(h, wdq_ref[...], preferred_element_type=F32)
    cq = cq * lax.rsqrt(jnp.mean(cq * cq, axis=-1, keepdims=True) + EPS) * qn_ref[...]
    cq = cq.astype(BF16)
    q = jnp.dot(cq, wq_ref[...], preferred_element_type=F32)
    qs = jnp.dot(cq, wqs_ref[...], preferred_element_type=F32)
    gq, gqs = gq_ref[...], gqs_ref[...]
    for hd in range(q_ref.shape[0]):
        qh = q[:, hd * HEAD_PAD:(hd + 1) * HEAD_PAD]
        rinv = lax.rsqrt(jnp.sum(qh * qh, axis=-1, keepdims=True) / QK_HEAD + EPS)
        qh = qh * rinv * gq
        qsw = qs[:, hd * ROPE_PAD:(hd + 1) * ROPE_PAD] * rinv * gqs
        rope = _rotate(qh[:, :ROPE_PAD], qsw, cos, sin)
        q_ref[hd] = jnp.concatenate([rope, qh[:, ROPE_PAD:]], axis=-1).astype(BF16)

    kv = jnp.dot(h, wkv_ref[...], preferred_element_type=F32)
    lora = ckv_ref.shape[1]
    c = kv[:, :lora]
    ckv_ref[...] = c * lax.rsqrt(jnp.mean(c * c, axis=-1, keepdims=True) + EPS) * kvn_ref[...]
    kpe_ref[...] = kv[:, lora:]
    pet_ref[...] = lax.dot_general(wpet_ref[...], h, NT_DIMS, preferred_element_type=F32)


def _mla_pre(x, mod, g, cos, sin, wdq, qn, wq, wqs, wkv, wpet, kvn, gq, gqs, n_prompt, rows_per_group):
    n, d = x.shape
    tm = min(MIX_TM, n)
    qlora = wdq.shape[1]
    lora = kvn.shape[1]
    heads = wq.shape[1] // HEAD_PAD
    const = lambda shape: pl.BlockSpec(shape, lambda i: (0,) * len(shape))
    row = lambda width: pl.BlockSpec((tm, width), lambda i: (i, 0))
    return pl.pallas_call(
        _mla_pre_kernel,
        grid=(n // tm,),
        in_specs=[
            row(d),
            pl.BlockSpec((None, 3, d), _group_index_map(tm, n_prompt, rows_per_group)),
            const((1, d)), row(ROPE_PAD), row(ROPE_PAD),
            const((d, qlora)), const((1, qlora)),
            const((qlora, heads * HEAD_PAD)), const((qlora, heads * ROPE_PAD)),
            const((d, lora + ROPE_PAD)), const((2 * ROPE_PAD, d)), const((1, lora)),
            const((1, HEAD_PAD)), const((1, ROPE_PAD)),
        ],
        out_specs=[
            pl.BlockSpec((heads, tm, HEAD_PAD), lambda i: (0, i, 0)),
            row(lora), row(ROPE_PAD),
            pl.BlockSpec((2 * ROPE_PAD, tm), lambda i: (0, i)),
        ],
        out_shape=[
            jax.ShapeDtypeStruct((heads, n, HEAD_PAD), BF16),
            jax.ShapeDtypeStruct((n, lora), F32),
            jax.ShapeDtypeStruct((n, ROPE_PAD), F32),
            jax.ShapeDtypeStruct((2 * ROPE_PAD, n), F32),
        ],
        compiler_params=_params(("arbitrary",)),
        name="mla_pre",
    )(x, mod, g, cos, sin, wdq, qn, wq, wqs, wkv, wpet, kvn, gq, gqs)


def _mla_expand_kernel(ckv_ref, pet_ref, cos_ref, sin_ref, wkt_ref, wv_ref, gk_ref, gks_ref,
                       kt_ref, v_ref):
    c = ckv_ref[...].astype(BF16)
    knt = lax.dot_general(wkt_ref[...], c, NT_DIMS, preferred_element_type=F32)
    v = jnp.dot(c, wv_ref[...], preferred_element_type=F32)
    gk, gks = gk_ref[...], gks_ref[...]
    pet, pst = pet_ref[:ROPE_PAD, :], pet_ref[ROPE_PAD:, :]
    rope = _rotate(pet * gk[:ROPE_PAD], pst * gks, cos_ref[...], sin_ref[...])
    pe_sq = jnp.sum(pet * pet, axis=0, keepdims=True)
    for hd in range(kt_ref.shape[0]):
        kn = knt[hd * QK_NOPE:(hd + 1) * QK_NOPE, :]
        rinv = lax.rsqrt((jnp.sum(kn * kn, axis=0, keepdims=True) + pe_sq) / QK_HEAD + EPS)
        kt_ref[hd] = jnp.concatenate([rope * rinv, kn * rinv * gk[ROPE_PAD:]], axis=0).astype(BF16)
        v_ref[hd] = v[:, hd * V_HEAD:(hd + 1) * V_HEAD].astype(BF16)


def _mla_expand(ckv, pet, cos_t, sin_t, wkt, wv, gk, gks):
    n, lora = ckv.shape
    tm = min(MIX_TM, n)
    heads = wv.shape[1] // V_HEAD
    const = lambda shape: pl.BlockSpec(shape, lambda i: (0,) * len(shape))
    col = lambda height: pl.BlockSpec((height, tm), lambda i: (0, i))
    return pl.pallas_call(
        _mla_expand_kernel,
        grid=(n // tm,),
        in_specs=[pl.BlockSpec((tm, lora), lambda i: (i, 0)), col(2 * ROPE_PAD), col(ROPE_PAD), col(ROPE_PAD),
                  const(wkt.shape), const(wv.shape), const((HEAD_PAD, 1)), const((ROPE_PAD, 1))],
        out_specs=[pl.BlockSpec((heads, HEAD_PAD, tm), lambda i: (0, 0, i)),
                   pl.BlockSpec((heads, tm, V_HEAD), lambda i: (0, i, 0))],
        out_shape=[jax.ShapeDtypeStruct((heads, HEAD_PAD, n), BF16),
                   jax.ShapeDtypeStruct((heads, n, V_HEAD), BF16)],
        compiler_params=_params(("arbitrary",)),
        name="mla_expand",
    )(ckv, pet, cos_t, sin_t, wkt, wv, gk, gks)


def _attn_kernel(*refs, n_parts, scale):
    q_ref, o_ref = refs[0], refs[-1]
    c = scale * np.log2(np.e)
    sub = min(ATT_SUB, q_ref.shape[0])
    chunks = []
    for p in range(n_parts):
        kt_ref, v_ref = refs[1 + 2 * p], refs[2 + 2 * p]
        tk = min(ATT_TK, v_ref.shape[0])
        chunks += [(kt_ref, v_ref, k0, tk) for k0 in range(0, v_ref.shape[0], tk)]
    def scores(r0):
        q = q_ref[r0:r0 + sub, :]
        return [jnp.dot(q, kt_ref[:, k0:k0 + tk], preferred_element_type=F32) for kt_ref, _, k0, tk in chunks]

    tq = q_ref.shape[0]
    pending = [scores(r) for r in range(0, min(ATT_AHEAD * sub, tq), sub)]
    for r0 in range(0, tq, sub):
        s = pending.pop(0)
        if r0 + ATT_AHEAD * sub < tq:
            pending.append(scores(r0 + ATT_AHEAD * sub))
        m = functools.reduce(jnp.maximum, [jnp.max(sp, axis=-1, keepdims=True) for sp in s])
        l, acc = 0.0, 0.0
        for sp, (_, v_ref, k0, tk) in zip(s, chunks):
            e = jnp.exp2((sp - m) * c)
            l = l + jnp.sum(e, axis=-1, keepdims=True)
            acc = acc + jnp.dot(e.astype(BF16), v_ref[k0:k0 + tk, :], preferred_element_type=F32)
        o_ref[r0:r0 + sub, :] = (acc / l).astype(BF16)


def _attention(q, parts, q_row0, n_batch, t_q, out_rows):
    heads = q.shape[0]
    tq = min(ATT_TQ, t_q)
    nq = t_q // tq
    qb0 = q_row0 // tq
    in_specs = [pl.BlockSpec((None, tq, HEAD_PAD), lambda b, h, i: (h, qb0 + b * nq + i, 0))]
    args = [q]
    for kt, v, row0, t_k in parts:
        kb0 = row0 // t_k
        in_specs.append(pl.BlockSpec((None, HEAD_PAD, t_k), lambda b, h, i, kb0=kb0: (h, 0, kb0 + b)))
        in_specs.append(pl.BlockSpec((None, t_k, V_HEAD), lambda b, h, i, kb0=kb0: (h, kb0 + b, 0)))
        args += [kt, v]
    return pl.pallas_call(
        functools.partial(_attn_kernel, n_parts=len(parts), scale=QK_HEAD ** -0.5),
        grid=(n_batch, heads, nq),
        in_specs=in_specs,
        out_specs=pl.BlockSpec((tq, V_HEAD), lambda b, h, i: (b * nq + i, h)),
        out_shape=jax.ShapeDtypeStruct((out_rows, heads * V_HEAD), BF16),
        compiler_params=_params(("arbitrary", "arbitrary", "arbitrary")),
        name="attn",
    )(*args)


def _proj_res_kernel(x_ref, a_ref, mod_ref, w_ref, o_ref):
    y = jnp.dot(a_ref[...], w_ref[...], preferred_element_type=F32)
    o_ref[...] = x_ref[...] + mod_ref[2:3, :] * y


def _proj_res(x, a, mod, w, n_prompt, rows_per_group):
    n, d = x.shape
    tm = min(MIX_TM, n)
    return pl.pallas_call(
        _proj_res_kernel,
        grid=(n // tm,),
        in_specs=[
            pl.BlockSpec((tm, d), lambda i: (i, 0)),
            pl.BlockSpec((tm, a.shape[1]), lambda i: (i, 0)),
            pl.BlockSpec((None, 3, d), _group_index_map(tm, n_prompt, rows_per_group)),
            pl.BlockSpec(w.shape, lambda i: (0, 0)),
        ],
        out_specs=pl.BlockSpec((tm, d), lambda i: (i, 0)),
        out_shape=jax.ShapeDtypeStruct((n, d), F32),
        compiler_params=_params(("arbitrary",)),
        name="proj_res",
    )(x, a, mod, w)


def _glu_kernel(x_ref, mod_ref, g_ref, wa_ref, wb_ref, ba_ref, bb_ref, o_ref, h_ref):
    @pl.when(pl.program_id(1) == 0)
    def _():
        h = _norm_mod(x_ref[...], g_ref[...], mod_ref[0:1, :], mod_ref[1:2, :])
        h_ref[...] = h.astype(BF16)

    h = h_ref[...]
    a = jnp.dot(h, wa_ref[...], preferred_element_type=F32) + ba_ref[...]
    b = jnp.dot(h, wb_ref[...], preferred_element_type=F32) + bb_ref[...]
    o_ref[...] = a * jax.nn.sigmoid(b)


def _glu(x, mod, g, w1, b1, n_prompt, rows_per_group):
    n, d = x.shape
    tm, tn = min(GLU_TM, n), min(CONV_TN, d)
    nj = d // tn
    return pl.pallas_call(
        _glu_kernel,
        grid=(n // tm, nj),
        in_specs=[
            pl.BlockSpec((tm, d), lambda i, j: (i, 0)),
            pl.BlockSpec((None, 3, d), _group_index_map(tm, n_prompt, rows_per_group)),
            pl.BlockSpec((1, d), lambda i, j: (0, 0)),
            pl.BlockSpec((d, tn), lambda i, j: (0, j)),
            pl.BlockSpec((d, tn), lambda i, j: (0, nj + j)),
            pl.BlockSpec((1, tn), lambda i, j: (0, j)),
            pl.BlockSpec((1, tn), lambda i, j: (0, nj + j)),
        ],
        out_specs=pl.BlockSpec((tm, tn), lambda i, j: (i, j)),
        out_shape=jax.ShapeDtypeStruct((n, d), F32),
        scratch_shapes=[pltpu.VMEM((tm, d), BF16)],
        compiler_params=_params(("arbitrary", "arbitrary")),
        name="glu",
    )(x, mod, g, w1, w1, b1, b1)


def _conv_kernel(pos_ref, len_ref, x_ref, u_ref, up_ref, un_ref, mod_ref, dw_ref, dwb_ref,
                 lg_ref, lb_ref, w_ref, b_ref, o_ref, e_ref, c_ref):
    i = pl.program_id(0)
    tm, d = x_ref.shape
    halo = up_ref.shape[0]
    width = dw_ref.shape[0]
    pos0 = pos_ref[i]
    seqlen = len_ref[i]

    has_prev = (pos0 > 0).astype(F32)
    has_next = (pos0 + tm < seqlen).astype(F32)
    e_ref[0:halo, :] = up_ref[...] * has_prev
    e_ref[halo:halo + tm, :] = u_ref[...]
    e_ref[halo + tm:, :] = un_ref[...] * has_next

    rb, cb = min(CONV_RB, tm), min(CONV_CB, d)
    first = halo - width // 2
    for c0 in range(0, d, cb):
        cols = slice(c0, c0 + cb)
        for r0 in range(0, tm, rb):
            acc = jnp.zeros((rb, cb), F32) + dwb_ref[:, cols]
            for sp in range(SUBLANES):
                g = None
                for k in range(width):
                    if (first + k) % SUBLANES != sp:
                        continue
                    base = r0 + first + k - sp
                    term = e_ref[base:base + rb + SUBLANES, cols] * dw_ref[k:k + 1, cols]
                    g = term if g is None else g + term
                if g is not None:
                    acc = acc + g[sp:sp + rb, :]
            c_ref[r0:r0 + rb, cols] = acc

    c = c_ref[...]
    mu = jnp.mean(c, axis=-1, keepdims=True)
    var = jnp.mean(jnp.square(c - mu), axis=-1, keepdims=True)
    y = (c - mu) * lax.rsqrt(var + EPS) * lg_ref[...] + lb_ref[...]
    y = _silu(y).astype(BF16)
    y = jnp.dot(y, w_ref[...], preferred_element_type=F32) + b_ref[...]
    o_ref[...] = x_ref[...] + mod_ref[2:3, :] * y


def _conv(x, u, mod, dw, dw_b, ln_g, ln_b, w2, b2, pos0, seqlen, n_prompt, rows_per_group):
    n, d = x.shape
    tm = min(MIX_TM, n)
    prev, nxt = _halo_specs(tm, CONV_HALO, d, n)
    const = lambda shape: pl.BlockSpec(shape, lambda i, *_: (0,) * len(shape))
    grid_spec = pltpu.PrefetchScalarGridSpec(
        num_scalar_prefetch=2,
        grid=(n // tm,),
        in_specs=[
            pl.BlockSpec((tm, d), lambda i, *_: (i, 0)),
            pl.BlockSpec((tm, d), lambda i, *_: (i, 0)),
            prev, nxt,
            pl.BlockSpec((None, 3, d), _group_index_map(tm, n_prompt, rows_per_group)),
            const(dw.shape), const((1, d)), const((1, d)), const((1, d)),
            const(w2.shape), const((1, d)),
        ],
        out_specs=pl.BlockSpec((tm, d), lambda i, *_: (i, 0)),
        scratch_shapes=[pltpu.VMEM((tm + 2 * CONV_HALO, d), F32), pltpu.VMEM((tm, d), F32)],
    )
    return pl.pallas_call(
        _conv_kernel,
        grid_spec=grid_spec,
        out_shape=jax.ShapeDtypeStruct((n, d), F32),
        compiler_params=_params(("arbitrary",)),
        name="conv",
    )(pos0, seqlen, x, u, u, u, mod, dw, dw_b, ln_g, ln_b, w2, b2)


def _rope_swap(a):
    q = QK_ROPE // 4
    lead = a.shape[:-1]
    return a.reshape(lead + (2, 2, q))[..., ::-1, :].reshape(lead + (QK_ROPE,))


def _pad_rope(a):
    return jnp.pad(a, [(0, 0)] * (a.ndim - 1) + [(0, ROPE_PAD - QK_ROPE)])


def _rope_tables(t):
    rows = t // GRID_W
    row = jnp.repeat(jnp.arange(rows), GRID_W).astype(F32)
    col = jnp.tile(jnp.arange(GRID_W), rows).astype(F32)
    half = QK_ROPE // 2
    inv_freq = ROPE_BASE ** (-jnp.arange(0, half, 2, dtype=F32) / half)
    ang_r, ang_c = row[:, None] * inv_freq, col[:, None] * inv_freq
    cos = jnp.concatenate([jnp.cos(ang_r)] * 2 + [jnp.cos(ang_c)] * 2, axis=-1)
    sin = jnp.concatenate([-jnp.sin(ang_r), jnp.sin(ang_r), -jnp.sin(ang_c), jnp.sin(ang_c)], axis=-1)
    pad = ROPE_PAD - QK_ROPE
    cos = jnp.concatenate([cos, jnp.ones((t, pad), F32)], axis=-1)
    sin = jnp.concatenate([sin, jnp.zeros((t, pad), F32)], axis=-1)
    return cos, sin


def _head_layout(w, heads):
    rows = w.shape[0]
    w = w.reshape(rows, heads, QK_HEAD)
    nope, rope = w[..., :QK_NOPE], w[..., QK_NOPE:]
    main = jnp.concatenate([_pad_rope(rope), nope], axis=-1).reshape(rows, heads * HEAD_PAD)
    swapped = _pad_rope(_rope_swap(rope)).reshape(rows, heads * ROPE_PAD)
    return main, swapped


def _gain_layout(gain):
    nope, rope = gain[:QK_NOPE], gain[QK_NOPE:]
    main = jnp.concatenate([_pad_rope(rope), nope])[None, :]
    return main, _pad_rope(_rope_swap(rope))[None, :]


def kernel(x_prompt, x_sample, cache_ckv, cache_kpe, c, c_ctx, norm_g, w_ada, b_ada, ffn_w1, ffn_w3, ffn_w2, pool_w, pool_scale, mla_w_dq, mla_q_norm, mla_w_uq, mla_w_dkv, mla_kv_norm, mla_w_ukv, mla_w_o, mla_q_gain, mla_k_gain, conv_w1, conv_b1, conv_dw, conv_dw_b, conv_ln_g, conv_ln_b, conv_w2, conv_b2):
    batch, seq, d = x_prompt.shape
    dec_batch, dec_seq, _ = x_sample.shape
    depth = norm_g.shape[0]
    past_len = cache_ckv.shape[2]
    lora = mla_kv_norm.shape[1]
    heads = MLA_HEADS
    n_prompt = batch * seq
    n_sample = dec_batch * dec_seq
    n = n_prompt + n_sample
    rows_per_group = dec_seq
    for tm in (FFN_TM, MIX_TM):
        assert n_prompt % tm == 0 and dec_seq % tm == 0
    assert seq % MIX_TM == 0 and n_prompt % dec_seq == 0 and dec_seq % GRID_W == 0
    state_ckv, state_kpe = [], []

    x = jnp.concatenate([x_prompt.reshape(n_prompt, d), x_sample.reshape(n_sample, d)], axis=0)
    cond = jnp.concatenate([c_ctx[None, :], c], axis=0)
    groups = cond.shape[0]
    mod_all = _ada(cond, w_ada, b_ada).reshape(depth, groups, N_MOD // 3, 3, d)

    pos0, seqlen = _tile_positions(min(MIX_TM, n), n_prompt, seq, n, dec_seq)

    w13, w2 = _ffn_weights(ffn_w1, ffn_w3, ffn_w2)

    for i in range(depth):
        mods = [mod_all[i, :, s] for s in range(3)]
        ng = lambda s: norm_g[i, s][None, :]
        x = _ffn(x, mods[0], ng(0), w13, w2, i, 0, n_prompt, rows_per_group)
        kind, j = i % N_MIXERS, i // N_MIXERS
        if kind == 0:
            x = _pool(x, mods[1], ng(1), pool_w[j].astype(BF16), pool_scale[j][None, :], pos0, seqlen,
                      n_prompt, rows_per_group)
        elif kind == 1:
            cos_t, sin_t = _rope_tables(dec_seq)
            cos = jnp.concatenate([jnp.ones((n_prompt, ROPE_PAD), F32)] + [cos_t] * dec_batch, axis=0)
            sin = jnp.concatenate([jnp.zeros((n_prompt, ROPE_PAD), F32)] + [sin_t] * dec_batch, axis=0)
            wq, wqs = _head_layout(mla_w_uq[j], heads)
            wd = mla_w_dkv[j]
            wpe = _pad_rope(wd[:, lora:])
            wkv = jnp.concatenate([wd[:, :lora], wpe], axis=-1)
            wpet = jnp.concatenate([wpe, _pad_rope(_rope_swap(wd[:, lora:]))], axis=-1).T
            gq, gqs = _gain_layout(mla_q_gain[j])
            gk, gks = _gain_layout(mla_k_gain[j])
            q, ckv, kpe, pet = _mla_pre(
                x, mods[1], ng(1), cos, sin, mla_w_dq[j].astype(BF16), mla_q_norm[j][None, :],
                wq.astype(BF16), wqs.astype(BF16), wkv.astype(BF16), wpet.astype(BF16),
                mla_kv_norm[j][None, :], gq, gqs, n_prompt, rows_per_group)
            state_ckv.append(ckv[:n_prompt].reshape(batch, seq, lora))
            state_kpe.append(kpe[:n_prompt, :QK_ROPE].reshape(batch, seq, QK_ROPE))
            w_ukv = mla_w_ukv[j].reshape(lora, heads, QK_NOPE + V_HEAD)
            wkt = w_ukv[..., :QK_NOPE].reshape(lora, heads * QK_NOPE).T.astype(BF16)
            wv = w_ukv[..., QK_NOPE:].reshape(lora, heads * V_HEAD).astype(BF16)
            k_tok, v_tok = _mla_expand(ckv, pet, cos.T, sin.T, wkt, wv, gk.T, gks.T)
            n_cache = dec_batch * past_len
            pet_ctx = jnp.concatenate([_pad_rope(cache_kpe[:, j].reshape(n_cache, QK_ROPE)).T,
                                       jnp.zeros((ROPE_PAD, n_cache), F32)], axis=0)
            k_ctx, v_ctx = _mla_expand(
                cache_ckv[:, j].reshape(n_cache, lora), pet_ctx,
                jnp.ones((ROPE_PAD, n_cache), F32), jnp.zeros((ROPE_PAD, n_cache), F32),
                wkt, wv, gk.T, gks.T)
            o_prompt = _attention(q, [(k_tok, v_tok, 0, seq)], 0, batch, seq, n_prompt)
            o_sample = _attention(q, [(k_ctx, v_ctx, 0, past_len), (k_tok, v_tok, n_prompt, dec_seq)],
                                  n_prompt, dec_batch, dec_seq, n_sample)
            o = jnp.concatenate([o_prompt, o_sample], axis=0)
            x = _proj_res(x, o, mods[1], mla_w_o[j].astype(BF16), n_prompt, rows_per_group)
        else:
            u = _glu(x, mods[1], ng(1), conv_w1[j].astype(BF16), conv_b1[j][None, :], n_prompt, rows_per_group)
            x = _conv(x, u, mods[1], conv_dw[j], conv_dw_b[j][None, :], conv_ln_g[j][None, :],
                      conv_ln_b[j][None, :], conv_w2[j].astype(BF16), conv_b2[j][None, :],
                      pos0, seqlen, n_prompt, rows_per_group)
        if i + 1 < depth:
            x = _ffn(x, mods[2], ng(2), w13, w2, i, 1, n_prompt, rows_per_group)
        else:
            last = (mods[2], ng(2), w13, w2, i, 1, n_prompt, rows_per_group)
            y_prompt = _ffn(x, *last, row0=0, rows=n_prompt).reshape(batch, seq, d)
            y_sample = _ffn(x, *last, row0=n_prompt, rows=n_sample).reshape(dec_batch, dec_seq, d)

    return (y_prompt, y_sample, jnp.stack(state_ckv, axis=1), jnp.stack(state_kpe, axis=1))
```

```python
import functools

import numpy as np
import jax
import jax.numpy as jnp
from jax import lax
from jax.experimental import pallas as pl
from jax.experimental.pallas import tpu as pltpu

F32 = jnp.float32
BF16 = jnp.bfloat16

N_MIXERS = 3
N_MOD = 9
POOL_WINDOWS = (2, 4, 8, 16)
MLA_HEADS = 16
QK_NOPE = 128
QK_ROPE = 64
V_HEAD = 128
QK_HEAD = QK_NOPE + QK_ROPE
GRID_W = 64
ROPE_BASE = 10000.0
EPS = 1e-6

SUBLANES = 8
LANES = 128
VMEM_LIMIT_BYTES = 56 * 1024 * 1024

ROPE_PAD = LANES
HEAD_PAD = ROPE_PAD + QK_NOPE

FFN_TM = 1024
FFN_SUB = 512
FFN_SUB0 = 256
FFN_TH = 512
FFN_VMEM_BYTES = 60 * 1024 * 1024
MIX_TM = 256
ADA_TN = 1024
GLU_TM = 512
CONV_TN = 1024
ATT_TQ = 1024
ATT_KEYS = 4096
ATT_TK = 1024
ATT_SUB = 256
POOL_HALO = 8
CONV_HALO = 16
CONV_RB = 128
CONV_CB = 128


def _params(sem, vmem_limit_bytes=VMEM_LIMIT_BYTES):
    return pltpu.CompilerParams(dimension_semantics=sem, vmem_limit_bytes=vmem_limit_bytes)


def _norm_mod(x, g, shift, scale):
    y = x * lax.rsqrt(jnp.mean(x * x, axis=-1, keepdims=True) + EPS) * g
    return y * (1.0 + scale) + shift


def _silu(u):
    return u * jax.nn.sigmoid(u)


def _group_index_map(tm, n_prompt, rows_per_group, row0=0):
    def index_map(i, *_):
        row = row0 + i * tm
        grp = jnp.where(row < n_prompt, 0, 1 + (row - n_prompt) // rows_per_group)
        return (grp, 0, 0)
    return index_map


def _ada_kernel(c_ref, w_ref, b_ref, o_ref):
    a = _silu(c_ref[...]).astype(BF16)
    o_ref[...] = jnp.dot(a, w_ref[...].astype(BF16), preferred_element_type=F32) + b_ref[...]


def _ada(cond, w_ada, b_ada):
    depth, d, nout = w_ada.shape
    groups = cond.shape[0]
    g = -(-groups // SUBLANES) * SUBLANES
    cond = jnp.pad(cond, ((0, g - groups), (0, 0)))
    tn = min(ADA_TN, nout)
    out = pl.pallas_call(
        _ada_kernel,
        grid=(depth, nout // tn),
        in_specs=[
            pl.BlockSpec((g, d), lambda l, j: (0, 0)),
            pl.BlockSpec((None, d, tn), lambda l, j: (l, 0, j)),
            pl.BlockSpec((None, 1, tn), lambda l, j: (l, 0, j)),
        ],
        out_specs=pl.BlockSpec((None, g, tn), lambda l, j: (l, 0, j)),
        out_shape=jax.ShapeDtypeStruct((depth, g, nout), F32),
        compiler_params=_params(("arbitrary", "arbitrary")),
        name="ada",
    )(cond, w_ada, b_ada.reshape(depth, 1, nout))
    return out[:, :groups]


def _ffn_kernel(x_ref, mod_ref, g_ref, w13_ref, w2_ref, o_ref, h_ref):
    j = pl.program_id(1)
    tm = o_ref.shape[0]
    th = w2_ref.shape[0]
    gate = 0.5 * mod_ref[2:3, :]

    def gated_swiglu(h):
        uv = jnp.dot(h, w13_ref[...], preferred_element_type=F32)
        a = (_silu(uv[:, :th]) * uv[:, th:]).astype(BF16)
        return gate * jnp.dot(a, w2_ref[...], preferred_element_type=F32)

    @pl.when(j == 0)
    def _():
        sub = min(FFN_SUB0, tm)
        gs = g_ref[...] * (1.0 + mod_ref[1:2, :])
        shift = mod_ref[0:1, :]

        def norm(r0):
            x = x_ref[r0:r0 + sub, :]
            h = x * lax.rsqrt(jnp.mean(x * x, axis=-1, keepdims=True) + EPS) * gs + shift
            h = h.astype(BF16)
            h_ref[r0:r0 + sub, :] = h
            return x, h

        pending = norm(0)
        for r0 in range(0, tm, sub):
            x, h = pending
            if r0 + sub < tm:
                pending = norm(r0 + sub)
            o_ref[r0:r0 + sub, :] = x + gated_swiglu(h)

    @pl.when(j > 0)
    def _():
        sub = min(FFN_SUB, tm)
        for r0 in range(0, tm, sub):
            o_ref[r0:r0 + sub, :] += gated_swiglu(h_ref[r0:r0 + sub, :])


def _ffn_prep_kernel(w1_ref, w3_ref, w2_ref, o13_ref, o2_ref):
    th = w1_ref.shape[1]
    o13_ref[:, :th] = w1_ref[...].astype(BF16)
    o13_ref[:, th:] = w3_ref[...].astype(BF16)
    o2_ref[...] = w2_ref[...].astype(BF16)


def _ffn_weights(w1, w3, w2):
    depth, halves, d, hid = w1.shape
    th = min(FFN_TH, hid)
    nj = hid // th
    return pl.pallas_call(
        _ffn_prep_kernel,
        grid=(depth, halves, nj),
        in_specs=[
            pl.BlockSpec((None, None, d, th), lambda l, s, j: (l, s, 0, j)),
            pl.BlockSpec((None, None, d, th), lambda l, s, j: (l, s, 0, j)),
            pl.BlockSpec((None, None, th, d), lambda l, s, j: (l, s, j, 0)),
        ],
        out_specs=[
            pl.BlockSpec((None, None, None, d, 2 * th), lambda l, s, j: (l, s, j, 0, 0)),
            pl.BlockSpec((None, None, th, d), lambda l, s, j: (l, s, j, 0)),
        ],
        out_shape=[
            jax.ShapeDtypeStruct((depth, halves, nj, d, 2 * th), BF16),
            jax.ShapeDtypeStruct((depth, halves, hid, d), BF16),
        ],
        compiler_params=_params(("arbitrary", "arbitrary", "arbitrary")),
        name="ffn_prep",
    )(w1, w3, w2)


def _ffn(x, mod, g, w13, w2, layer, half, n_prompt, rows_per_group, row0=0, rows=None):
    d = x.shape[1]
    n = x.shape[0] if rows is None else rows
    nj, _, th2 = w13.shape[2:]
    th = th2 // 2
    tm = min(FFN_TM, n)
    tile0 = row0 // tm
    return pl.pallas_call(
        _ffn_kernel,
        grid=(n // tm, nj),
        in_specs=[
            pl.BlockSpec((tm, d), lambda i, j: (tile0 + i, 0)),
            pl.BlockSpec((None, 3, d), _group_index_map(tm, n_prompt, rows_per_group, row0)),
            pl.BlockSpec((1, d), lambda i, j: (0, 0)),
            pl.BlockSpec((None, None, None, d, th2), lambda i, j: (layer, half, j, 0, 0)),
            pl.BlockSpec((None, None, th, d), lambda i, j: (layer, half, j, 0)),
        ],
        out_specs=pl.BlockSpec((tm, d), lambda i, j: (i, 0)),
        out_shape=jax.ShapeDtypeStruct((n, d), F32),
        scratch_shapes=[pltpu.VMEM((tm, d), BF16)],
        compiler_params=_params(("arbitrary", "arbitrary"), FFN_VMEM_BYTES),
        name="ffn",
    )(x, mod, g, w13, w2)


def _tile_positions(tm, n_prompt, seq, n, dec_seq):
    starts = np.arange(0, n, tm)
    is_prompt = starts < n_prompt
    pos0 = np.where(is_prompt, starts % seq, (starts - n_prompt) % dec_seq)
    length = np.where(is_prompt, seq, dec_seq)
    return jnp.asarray(pos0, jnp.int32), jnp.asarray(length, jnp.int32)


def _halo_specs(tm, halo, d, n):
    per = tm // halo
    last = n // halo - 1
    prev = pl.BlockSpec((halo, d), lambda i, *_: (jnp.maximum(i * per - 1, 0), 0))
    nxt = pl.BlockSpec((halo, d), lambda i, *_: (jnp.minimum((i + 1) * per, last), 0))
    return prev, nxt


def _pool_kernel(pos_ref, len_ref, x_ref, xp_ref, xn_ref, mod_ref, g_ref, w_ref, sc_ref, o_ref, e_ref):
    i = pl.program_id(0)
    tm, d = x_ref.shape
    halo = xp_ref.shape[0]
    pos0 = pos_ref[i]
    seqlen = len_ref[i]
    g, shift, scale = g_ref[...], mod_ref[0:1, :], mod_ref[1:2, :]

    has_prev = (pos0 > 0).astype(F32)
    has_next = (pos0 + tm < seqlen).astype(F32)
    e_ref[0:halo, :] = _norm_mod(xp_ref[...], g, shift, scale) * has_prev
    e_ref[halo:halo + tm, :] = _norm_mod(x_ref[...], g, shift, scale)
    e_ref[halo + tm:, :] = _norm_mod(xn_ref[...], g, shift, scale) * has_next

    rows = e_ref.shape[0]
    pos = pos0 + lax.broadcasted_iota(jnp.int32, (tm, 1), 0)
    gw = d // len(POOL_WINDOWS)
    for gi, w in enumerate(POOL_WINDOWS):
        cols = slice(gi * gw, (gi + 1) * gw)
        e = e_ref[:, cols]
        s = pltpu.roll(e, 1, 0) + e
        half = 1
        while 2 * half < w:
            s = pltpu.roll(s, half, 0) + pltpu.roll(s, rows - half, 0)
            half *= 2
        s = s[halo:halo + tm, :]
        hi = jnp.minimum(pos + (w // 2 - 1), seqlen - 1)
        lo = jnp.maximum(pos - w // 2, 0)
        cnt = (hi - lo + 1).astype(F32)
        diff = (s / cnt - e[halo:halo + tm, :]).astype(BF16)
        y = jnp.dot(diff, w_ref[gi], preferred_element_type=F32) * sc_ref[:, cols]
        o_ref[:, cols] = x_ref[:, cols] + mod_ref[2:3, cols] * y


def _pool(x, mod, g, w_pool, pool_scale, pos0, seqlen, n_prompt, rows_per_group):
    n, d = x.shape
    tm = min(MIX_TM, n)
    prev, nxt = _halo_specs(tm, POOL_HALO, d, n)
    ng, gw, _ = w_pool.shape
    grid_spec = pltpu.PrefetchScalarGridSpec(
        num_scalar_prefetch=2,
        grid=(n // tm,),
        in_specs=[
            pl.BlockSpec((tm, d), lambda i, *_: (i, 0)),
            prev, nxt,
            pl.BlockSpec((None, 3, d), _group_index_map(tm, n_prompt, rows_per_group)),
            pl.BlockSpec((1, d), lambda i, *_: (0, 0)),
            pl.BlockSpec((ng, gw, gw), lambda i, *_: (0, 0, 0)),
            pl.BlockSpec((1, d), lambda i, *_: (0, 0)),
        ],
        out_specs=pl.BlockSpec((tm, d), lambda i, *_: (i, 0)),
        scratch_shapes=[pltpu.VMEM((tm + 2 * POOL_HALO, d), F32)],
    )
    return pl.pallas_call(
        _pool_kernel,
        grid_spec=grid_spec,
        out_shape=jax.ShapeDtypeStruct((n, d), F32),
        compiler_params=_params(("arbitrary",)),
        name="pool",
    )(pos0, seqlen, x, x, x, mod, g, w_pool, pool_scale)


def _rotate(main, swapped, cos, sin):
    return main * cos + swapped * sin


NT_DIMS = (((1,), (1,)), ((), ()))


def _mla_pre_kernel(x_ref, mod_ref, g_ref, cos_ref, sin_ref, wdq_ref, qn_ref, wq_ref, wqs_ref,
                    wkv_ref, wpet_ref, kvn_ref, gq_ref, gqs_ref, q_ref, ckv_ref, kpe_ref, pet_ref):
    h = _norm_mod(x_ref[...], g_ref[...], mod_ref[0:1, :], mod_ref[1:2, :]).astype(BF16)
    cos, sin = cos_ref[...], sin_ref[...]

    cq = jnp.dot(h, wdq_ref[...], preferred_element_type=F32)
    cq = cq * lax.rsqrt(jnp.mean(cq * cq, axis=-1, keepdims=True) + EPS) * qn_ref[...]
    cq = cq.astype(BF16)
    q = jnp.dot(cq, wq_ref[...], preferred_element_type=F32)
    qs = jnp.dot(cq, wqs_ref[...], preferred_element_type=F32)
    gq, gqs = gq_ref[...], gqs_ref[...]
    for hd in range(q_ref.shape[0]):
        qh = q[:, hd * HEAD_PAD:(hd + 1) * HEAD_PAD]
        rinv = lax.rsqrt(jnp.sum(qh * qh, axis=-1, keepdims=True) / QK_HEAD + EPS)
        qh = qh * rinv * gq
        qsw = qs[:, hd * ROPE_PAD:(hd + 1) * ROPE_PAD] * rinv * gqs
        rope = _rotate(qh[:, :ROPE_PAD], qsw, cos, sin)
        q_ref[hd] = jnp.concatenate([rope, qh[:, ROPE_PAD:]], axis=-1).astype(BF16)

    kv = jnp.dot(h, wkv_ref[...], preferred_element_type=F32)
    lora = ckv_ref.shape[1]
    c = kv[:, :lora]
    ckv_ref[...] = c * lax.rsqrt(jnp.mean(c * c, axis=-1, keepdims=True) + EPS) * kvn_ref[...]
    kpe_ref[...] = kv[:, lora:]
    pet_ref[...] = lax.dot_general(wpet_ref[...], h, NT_DIMS, preferred_element_type=F32)


def _mla_pre(x, mod, g, cos, sin, wdq, qn, wq, wqs, wkv, wpet, kvn, gq, gqs, n_prompt, rows_per_group):
    n, d = x.shape
    tm = min(MIX_TM, n)
    qlora = wdq.shape[1]
    lora = kvn.shape[1]
    heads = wq.shape[1] // HEAD_PAD
    const = lambda shape: pl.BlockSpec(shape, lambda i: (0,) * len(shape))
    row = lambda width: pl.BlockSpec((tm, width), lambda i: (i, 0))
    return pl.pallas_call(
        _mla_pre_kernel,
        grid=(n // tm,),
        in_specs=[
            row(d),
            pl.BlockSpec((None, 3, d), _group_index_map(tm, n_prompt, rows_per_group)),
            const((1, d)), row(ROPE_PAD), row(ROPE_PAD),
            const((d, qlora)), const((1, qlora)),
            const((qlora, heads * HEAD_PAD)), const((qlora, heads * ROPE_PAD)),
            const((d, lora + ROPE_PAD)), const((2 * ROPE_PAD, d)), const((1, lora)),
            const((1, HEAD_PAD)), const((1, ROPE_PAD)),
        ],
        out_specs=[
            pl.BlockSpec((heads, tm, HEAD_PAD), lambda i: (0, i, 0)),
            row(lora), row(ROPE_PAD),
            pl.BlockSpec((2 * ROPE_PAD, tm), lambda i: (0, i)),
        ],
        out_shape=[
            jax.ShapeDtypeStruct((heads, n, HEAD_PAD), BF16),
            jax.ShapeDtypeStruct((n, lora), F32),
            jax.ShapeDtypeStruct((n, ROPE_PAD), F32),
            jax.ShapeDtypeStruct((2 * ROPE_PAD, n), F32),
        ],
        compiler_params=_params(("arbitrary",)),
        name="mla_pre",
    )(x, mod, g, cos, sin, wdq, qn, wq, wqs, wkv, wpet, kvn, gq, gqs)


def _mla_expand_kernel(ckv_ref, pet_ref, cos_ref, sin_ref, wkt_ref, wv_ref, gk_ref, gks_ref,
                       kt_ref, v_ref):
    c = ckv_ref[...].astype(BF16)
    knt = lax.dot_general(wkt_ref[...], c, NT_DIMS, preferred_element_type=F32)
    v = jnp.dot(c, wv_ref[...], preferred_element_type=F32)
    gk, gks = gk_ref[...], gks_ref[...]
    pet, pst = pet_ref[:ROPE_PAD, :], pet_ref[ROPE_PAD:, :]
    rope = _rotate(pet * gk[:ROPE_PAD], pst * gks, cos_ref[...], sin_ref[...])
    pe_sq = jnp.sum(pet * pet, axis=0, keepdims=True)
    for hd in range(kt_ref.shape[0]):
        kn = knt[hd * QK_NOPE:(hd + 1) * QK_NOPE, :]
        rinv = lax.rsqrt((jnp.sum(kn * kn, axis=0, keepdims=True) + pe_sq) / QK_HEAD + EPS)
        kt_ref[hd] = jnp.concatenate([rope * rinv, kn * rinv * gk[ROPE_PAD:]], axis=0).astype(BF16)
        v_ref[hd] = v[:, hd * V_HEAD:(hd + 1) * V_HEAD].astype(BF16)


def _mla_expand(ckv, pet, cos_t, sin_t, wkt, wv, gk, gks):
    n, lora = ckv.shape
    tm = min(MIX_TM, n)
    heads = wv.shape[1] // V_HEAD
    const = lambda shape: pl.BlockSpec(shape, lambda i: (0,) * len(shape))
    col = lambda height: pl.BlockSpec((height, tm), lambda i: (0, i))
    return pl.pallas_call(
        _mla_expand_kernel,
        grid=(n // tm,),
        in_specs=[pl.BlockSpec((tm, lora), lambda i: (i, 0)), col(2 * ROPE_PAD), col(ROPE_PAD), col(ROPE_PAD),
                  const(wkt.shape), const(wv.shape), const((HEAD_PAD, 1)), const((ROPE_PAD, 1))],
        out_specs=[pl.BlockSpec((heads, HEAD_PAD, tm), lambda i: (0, 0, i)),
                   pl.BlockSpec((heads, tm, V_HEAD), lambda i: (0, i, 0))],
        out_shape=[jax.ShapeDtypeStruct((heads, HEAD_PAD, n), BF16),
                   jax.ShapeDtypeStruct((heads, n, V_HEAD), BF16)],
        compiler_params=_params(("arbitrary",)),
        name="mla_expand",
    )(ckv, pet, cos_t, sin_t, wkt, wv, gk, gks)


def _attn_kernel(*refs, n_parts, scale):
    for hd in range(refs[0].shape[0]):
        _attn_head(refs[0].at[hd], [(refs[1 + 2 * p].at[hd], refs[2 + 2 * p].at[hd]) for p in range(n_parts)],
                   refs[-1].at[:, hd * V_HEAD:(hd + 1) * V_HEAD], scale)


def _attn_head(q_ref, parts, o_ref, scale):
    c = scale * np.log2(np.e)
    sub = min(ATT_SUB, q_ref.shape[0])
    chunks = []
    for kt_ref, v_ref in parts:
        tk = min(ATT_TK, v_ref.shape[0])
        chunks += [(kt_ref, v_ref, k0, tk) for k0 in range(0, v_ref.shape[0], tk)]

    def score(r0, chunk):
        kt_ref, _, k0, tk = chunk
        return jnp.dot(q_ref[r0:r0 + sub, :], kt_ref[:, k0:k0 + tk], preferred_element_type=F32)

    tq = q_ref.shape[0]
    s = [score(0, chunk) for chunk in chunks]
    for r0 in range(0, tq, sub):
        m = functools.reduce(jnp.maximum, [jnp.max(sp, axis=-1, keepdims=True) for sp in s])
        s_next = []
        l, acc = 0.0, 0.0
        for sp, chunk in zip(s, chunks):
            if r0 + sub < tq:
                s_next.append(score(r0 + sub, chunk))
            _, v_ref, k0, tk = chunk
            e = jnp.exp2((sp - m) * c)
            l = l + jnp.sum(e, axis=-1, keepdims=True)
            acc = acc + jnp.dot(e.astype(BF16), v_ref[k0:k0 + tk, :], preferred_element_type=F32)
        o_ref[r0:r0 + sub, :] = (acc / l).astype(BF16)
        s = s_next


def _attention(q, parts, q_row0, n_batch, t_q, out_rows):
    heads = q.shape[0]
    tq = min(ATT_TQ, t_q)
    nq = t_q // tq
    qb0 = q_row0 // tq
    keys = sum(t_k for _, _, _, t_k in parts)
    hb = max(1, min(heads, (ATT_TQ * ATT_KEYS) // (tq * keys)))
    while heads % hb:
        hb -= 1
    in_specs = [pl.BlockSpec((hb, tq, HEAD_PAD), lambda b, h, i: (h, qb0 + b * nq + i, 0))]
    args = [q]
    for kt, v, row0, t_k in parts:
        kb0 = row0 // t_k
        in_specs.append(pl.BlockSpec((hb, HEAD_PAD, t_k), lambda b, h, i, kb0=kb0: (h, 0, kb0 + b)))
        in_specs.append(pl.BlockSpec((hb, t_k, V_HEAD), lambda b, h, i, kb0=kb0: (h, kb0 + b, 0)))
        args += [kt, v]
    return pl.pallas_call(
        functools.partial(_attn_kernel, n_parts=len(parts), scale=QK_HEAD ** -0.5),
        grid=(n_batch, heads // hb, nq),
        in_specs=in_specs,
        out_specs=pl.BlockSpec((tq, hb * V_HEAD), lambda b, h, i: (b * nq + i, h)),
        out_shape=jax.ShapeDtypeStruct((out_rows, heads * V_HEAD), BF16),
        compiler_params=_params(("arbitrary", "arbitrary", "arbitrary")),
        name="attn",
    )(*args)


def _proj_res_kernel(x_ref, a_ref, mod_ref, w_ref, o_ref):
    y = jnp.dot(a_ref[...], w_ref[...], preferred_element_type=F32)
    o_ref[...] = x_ref[...] + mod_ref[2:3, :] * y


def _proj_res(x, a, mod, w, n_prompt, rows_per_group):
    n, d = x.shape
    tm = min(MIX_TM, n)
    return pl.pallas_call(
        _proj_res_kernel,
        grid=(n // tm,),
        in_specs=[
            pl.BlockSpec((tm, d), lambda i: (i, 0)),
            pl.BlockSpec((tm, a.shape[1]), lambda i: (i, 0)),
            pl.BlockSpec((None, 3, d), _group_index_map(tm, n_prompt, rows_per_group)),
            pl.BlockSpec(w.shape, lambda i: (0, 0)),
        ],
        out_specs=pl.BlockSpec((tm, d), lambda i: (i, 0)),
        out_shape=jax.ShapeDtypeStruct((n, d), F32),
        compiler_params=_params(("arbitrary",)),
        name="proj_res",
    )(x, a, mod, w)


def _glu_kernel(x_ref, mod_ref, g_ref, wa_ref, wb_ref, ba_ref, bb_ref, o_ref, h_ref):
    j = pl.program_id(1)
    tm = x_ref.shape[0]

    def glu(h):
        a = jnp.dot(h, wa_ref[...], preferred_element_type=F32) + ba_ref[...]
        b = jnp.dot(h, wb_ref[...], preferred_element_type=F32) + bb_ref[...]
        return a * jax.nn.sigmoid(b)

    @pl.when(j == 0)
    def _():
        sub = min(FFN_SUB0, tm)

        def norm(r0):
            h = _norm_mod(x_ref[r0:r0 + sub, :], g_ref[...], mod_ref[0:1, :], mod_ref[1:2, :]).astype(BF16)
            h_ref[r0:r0 + sub, :] = h
            return h

        pending = norm(0)
        for r0 in range(0, tm, sub):
            h = pending
            if r0 + sub < tm:
                pending = norm(r0 + sub)
            o_ref[r0:r0 + sub, :] = glu(h)

    @pl.when(j > 0)
    def _():
        o_ref[...] = glu(h_ref[...])


def _glu(x, mod, g, w1, b1, n_prompt, rows_per_group):
    n, d = x.shape
    tm, tn = min(GLU_TM, n), min(CONV_TN, d)
    nj = d // tn
    return pl.pallas_call(
        _glu_kernel,
        grid=(n // tm, nj),
        in_specs=[
            pl.BlockSpec((tm, d), lambda i, j: (i, 0)),
            pl.BlockSpec((None, 3, d), _group_index_map(tm, n_prompt, rows_per_group)),
            pl.BlockSpec((1, d), lambda i, j: (0, 0)),
            pl.BlockSpec((d, tn), lambda i, j: (0, j)),
            pl.BlockSpec((d, tn), lambda i, j: (0, nj + j)),
            pl.BlockSpec((1, tn), lambda i, j: (0, j)),
            pl.BlockSpec((1, tn), lambda i, j: (0, nj + j)),
        ],
        out_specs=pl.BlockSpec((tm, tn), lambda i, j: (i, j)),
        out_shape=jax.ShapeDtypeStruct((n, d), F32),
        scratch_shapes=[pltpu.VMEM((tm, d), BF16)],
        compiler_params=_params(("arbitrary", "arbitrary")),
        name="glu",
    )(x, mod, g, w1, w1, b1, b1)


def _conv_kernel(pos_ref, len_ref, x_ref, u_ref, up_ref, un_ref, mod_ref, dw_ref, dwb_ref,
                 lg_ref, lb_ref, w_ref, b_ref, o_ref, e_ref, c_ref):
    i = pl.program_id(0)
    tm, d = x_ref.shape
    halo = up_ref.shape[0]
    width = dw_ref.shape[0]
    pos0 = pos_ref[i]
    seqlen = len_ref[i]

    has_prev = (pos0 > 0).astype(F32)
    has_next = (pos0 + tm < seqlen).astype(F32)
    e_ref[0:halo, :] = up_ref[...] * has_prev
    e_ref[halo:halo + tm, :] = u_ref[...]
    e_ref[halo + tm:, :] = un_ref[...] * has_next

    rb, cb = min(CONV_RB, tm), min(CONV_CB, d)
    first = halo - width // 2
    for c0 in range(0, d, cb):
        cols = slice(c0, c0 + cb)
        for r0 in range(0, tm, rb):
            acc = jnp.zeros((rb, cb), F32) + dwb_ref[:, cols]
            for sp in range(SUBLANES):
                g = None
                for k in range(width):
                    if (first + k) % SUBLANES != sp:
                        continue
                    base = r0 + first + k - sp
                    term = e_ref[base:base + rb + SUBLANES, cols] * dw_ref[k:k + 1, cols]
                    g = term if g is None else g + term
                if g is not None:
                    acc = acc + g[sp:sp + rb, :]
            c_ref[r0:r0 + rb, cols] = acc

    c = c_ref[...]
    mu = jnp.mean(c, axis=-1, keepdims=True)
    var = jnp.mean(jnp.square(c - mu), axis=-1, keepdims=True)
    y = (c - mu) * lax.rsqrt(var + EPS) * lg_ref[...] + lb_ref[...]
    y = _silu(y).astype(BF16)
    y = jnp.dot(y, w_ref[...], preferred_element_type=F32) + b_ref[...]
    o_ref[...] = x_ref[...] + mod_ref[2:3, :] * y


def _conv(x, u, mod, dw, dw_b, ln_g, ln_b, w2, b2, pos0, seqlen, n_prompt, rows_per_group):
    n, d = x.shape
    tm = min(MIX_TM, n)
    prev, nxt = _halo_specs(tm, CONV_HALO, d, n)
    const = lambda shape: pl.BlockSpec(shape, lambda i, *_: (0,) * len(shape))
    grid_spec = pltpu.PrefetchScalarGridSpec(
        num_scalar_prefetch=2,
        grid=(n // tm,),
        in_specs=[
            pl.BlockSpec((tm, d), lambda i, *_: (i, 0)),
            pl.BlockSpec((tm, d), lambda i, *_: (i, 0)),
            prev, nxt,
            pl.BlockSpec((None, 3, d), _group_index_map(tm, n_prompt, rows_per_group)),
            const(dw.shape), const((1, d)), const((1, d)), const((1, d)),
            const(w2.shape), const((1, d)),
        ],
        out_specs=pl.BlockSpec((tm, d), lambda i, *_: (i, 0)),
        scratch_shapes=[pltpu.VMEM((tm + 2 * CONV_HALO, d), F32), pltpu.VMEM((tm, d), F32)],
    )
    return pl.pallas_call(
        _conv_kernel,
        grid_spec=grid_spec,
        out_shape=jax.ShapeDtypeStruct((n, d), F32),
        compiler_params=_params(("arbitrary",)),
        name="conv",
    )(pos0, seqlen, x, u, u, u, mod, dw, dw_b, ln_g, ln_b, w2, b2)


def _rope_swap(a):
    q = QK_ROPE // 4
    lead = a.shape[:-1]
    return a.reshape(lead + (2, 2, q))[..., ::-1, :].reshape(lead + (QK_ROPE,))


def _pad_rope(a):
    return jnp.pad(a, [(0, 0)] * (a.ndim - 1) + [(0, ROPE_PAD - QK_ROPE)])


def _rope_tables(t):
    rows = t // GRID_W
    row = jnp.repeat(jnp.arange(rows), GRID_W).astype(F32)
    col = jnp.tile(jnp.arange(GRID_W), rows).astype(F32)
    half = QK_ROPE // 2
    inv_freq = ROPE_BASE ** (-jnp.arange(0, half, 2, dtype=F32) / half)
    ang_r, ang_c = row[:, None] * inv_freq, col[:, None] * inv_freq
    cos = jnp.concatenate([jnp.cos(ang_r)] * 2 + [jnp.cos(ang_c)] * 2, axis=-1)
    sin = jnp.concatenate([-jnp.sin(ang_r), jnp.sin(ang_r), -jnp.sin(ang_c), jnp.sin(ang_c)], axis=-1)
    pad = ROPE_PAD - QK_ROPE
    cos = jnp.concatenate([cos, jnp.ones((t, pad), F32)], axis=-1)
    sin = jnp.concatenate([sin, jnp.zeros((t, pad), F32)], axis=-1)
    return cos, sin


def _head_layout(w, heads):
    rows = w.shape[0]
    w = w.reshape(rows, heads, QK_HEAD)
    nope, rope = w[..., :QK_NOPE], w[..., QK_NOPE:]
    main = jnp.concatenate([_pad_rope(rope), nope], axis=-1).reshape(rows, heads * HEAD_PAD)
    swapped = _pad_rope(_rope_swap(rope)).reshape(rows, heads * ROPE_PAD)
    return main, swapped


def _gain_layout(gain):
    nope, rope = gain[:QK_NOPE], gain[QK_NOPE:]
    main = jnp.concatenate([_pad_rope(rope), nope])[None, :]
    return main, _pad_rope(_rope_swap(rope))[None, :]


def kernel(x_prompt, x_sample, cache_ckv, cache_kpe, c, c_ctx, norm_g, w_ada, b_ada, ffn_w1, ffn_w3, ffn_w2, pool_w, pool_scale, mla_w_dq, mla_q_norm, mla_w_uq, mla_w_dkv, mla_kv_norm, mla_w_ukv, mla_w_o, mla_q_gain, mla_k_gain, conv_w1, conv_b1, conv_dw, conv_dw_b, conv_ln_g, conv_ln_b, conv_w2, conv_b2):
    batch, seq, d = x_prompt.shape
    dec_batch, dec_seq, _ = x_sample.shape
    depth = norm_g.shape[0]
    past_len = cache_ckv.shape[2]
    lora = mla_kv_norm.shape[1]
    heads = MLA_HEADS
    n_prompt = batch * seq
    n_sample = dec_batch * dec_seq
    n = n_prompt + n_sample
    rows_per_group = dec_seq
    for tm in (FFN_TM, MIX_TM):
        assert n_prompt % tm == 0 and dec_seq % tm == 0
    assert seq % MIX_TM == 0 and n_prompt % dec_seq == 0 and dec_seq % GRID_W == 0
    state_ckv, state_kpe = [], []

    x = jnp.concatenate([x_prompt.reshape(n_prompt, d), x_sample.reshape(n_sample, d)], axis=0)
    cond = jnp.concatenate([c_ctx[None, :], c], axis=0)
    groups = cond.shape[0]
    mod_all = _ada(cond, w_ada, b_ada).reshape(depth, groups, N_MOD // 3, 3, d)

    pos0, seqlen = _tile_positions(min(MIX_TM, n), n_prompt, seq, n, dec_seq)

    w13, w2 = _ffn_weights(ffn_w1, ffn_w3, ffn_w2)

    for i in range(depth):
        mods = [mod_all[i, :, s] for s in range(3)]
        ng = lambda s: norm_g[i, s][None, :]
        x = _ffn(x, mods[0], ng(0), w13, w2, i, 0, n_prompt, rows_per_group)
        kind, j = i % N_MIXERS, i // N_MIXERS
        if kind == 0:
            x = _pool(x, mods[1], ng(1), pool_w[j].astype(BF16), pool_scale[j][None, :], pos0, seqlen,
                      n_prompt, rows_per_group)
        elif kind == 1:
            cos_t, sin_t = _rope_tables(dec_seq)
            cos = jnp.concatenate([jnp.ones((n_prompt, ROPE_PAD), F32)] + [cos_t] * dec_batch, axis=0)
            sin = jnp.concatenate([jnp.zeros((n_prompt, ROPE_PAD), F32)] + [sin_t] * dec_batch, axis=0)
            wq, wqs = _head_layout(mla_w_uq[j], heads)
            wd = mla_w_dkv[j]
            wpe = _pad_rope(wd[:, lora:])
            wkv = jnp.concatenate([wd[:, :lora], wpe], axis=-1)
            wpet = jnp.concatenate([wpe, _pad_rope(_rope_swap(wd[:, lora:]))], axis=-1).T
            gq, gqs = _gain_layout(mla_q_gain[j])
            gk, gks = _gain_layout(mla_k_gain[j])
            q, ckv, kpe, pet = _mla_pre(
                x, mods[1], ng(1), cos, sin, mla_w_dq[j].astype(BF16), mla_q_norm[j][None, :],
                wq.astype(BF16), wqs.astype(BF16), wkv.astype(BF16), wpet.astype(BF16),
                mla_kv_norm[j][None, :], gq, gqs, n_prompt, rows_per_group)
            state_ckv.append(ckv[:n_prompt].reshape(batch, seq, lora))
            state_kpe.append(kpe[:n_prompt, :QK_ROPE].reshape(batch, seq, QK_ROPE))
            w_ukv = mla_w_ukv[j].reshape(lora, heads, QK_NOPE + V_HEAD)
            wkt = w_ukv[..., :QK_NOPE].reshape(lora, heads * QK_NOPE).T.astype(BF16)
            wv = w_ukv[..., QK_NOPE:].reshape(lora, heads * V_HEAD).astype(BF16)
            k_tok, v_tok = _mla_expand(ckv, pet, cos.T, sin.T, wkt, wv, gk.T, gks.T)
            n_cache = dec_batch * past_len
            pet_ctx = jnp.concatenate([_pad_rope(cache_kpe[:, j].reshape(n_cache, QK_ROPE)).T,
                                       jnp.zeros((ROPE_PAD, n_cache), F32)], axis=0)
            k_ctx, v_ctx = _mla_expand(
                cache_ckv[:, j].reshape(n_cache, lora), pet_ctx,
                jnp.ones((ROPE_PAD, n_cache), F32), jnp.zeros((ROPE_PAD, n_cache), F32),
                wkt, wv, gk.T, gks.T)
            o_prompt = _attention(q, [(k_tok, v_tok, 0, seq)], 0, batch, seq, n_prompt)
            o_sample = _attention(q, [(k_ctx, v_ctx, 0, past_len), (k_tok, v_tok, n_prompt, dec_seq)],
                                  n_prompt, dec_batch, dec_seq, n_sample)
            o = jnp.concatenate([o_prompt, o_sample], axis=0)
            x = _proj_res(x, o, mods[1], mla_w_o[j].astype(BF16), n_prompt, rows_per_group)
        else:
            u = _glu(x, mods[1], ng(1), conv_w1[j].astype(BF16), conv_b1[j][None, :], n_prompt, rows_per_group)
            x = _conv(x, u, mods[1], conv_dw[j], conv_dw_b[j][None, :], conv_ln_g[j][None, :],
                      conv_ln_b[j][None, :], conv_w2[j].astype(BF16), conv_b2[j][None, :],
                      pos0, seqlen, n_prompt, rows_per_group)
        if i + 1 < depth:
            x = _ffn(x, mods[2], ng(2), w13, w2, i, 1, n_prompt, rows_per_group)
        else:
            last = (mods[2], ng(2), w13, w2, i, 1, n_prompt, rows_per_group)
            y_prompt = _ffn(x, *last, row0=0, rows=n_prompt).reshape(batch, seq, d)
            y_sample = _ffn(x, *last, row0=n_prompt, rows=n_sample).reshape(dec_batch, dec_seq, d)

    return (y_prompt, y_sample, jnp.stack(state_ckv, axis=1), jnp.stack(state_kpe, axis=1))
```

```python
import functools

import numpy as np
import jax
import jax.numpy as jnp
from jax import lax
from jax.experimental import pallas as pl
from jax.experimental.pallas import tpu as pltpu

F32 = jnp.float32
BF16 = jnp.bfloat16

N_MIXERS = 3
N_MOD = 9
POOL_WINDOWS = (2, 4, 8, 16)
MLA_HEADS = 16
QK_NOPE = 128
QK_ROPE = 64
V_HEAD = 128
QK_HEAD = QK_NOPE + QK_ROPE
GRID_W = 64
ROPE_BASE = 10000.0
EPS = 1e-6

SUBLANES = 8
LANES = 128
VMEM_LIMIT_BYTES = 56 * 1024 * 1024

ROPE_PAD = LANES
HEAD_PAD = ROPE_PAD + QK_NOPE

FFN_TM = 1024
FFN_SUB = 512
FFN_SUB0 = 256
FFN_TH = 512
FFN_VMEM_BYTES = 60 * 1024 * 1024
MIX_TM = 256
ADA_TN = 1024
GLU_TM = 512
CONV_TN = 1024
ATT_TQ = 1024
ATT_KEYS = 4096
ATT_TK = 1024
ATT_SUB = 256
POOL_HALO = 8
CONV_HALO = 16
CONV_RB = 128
CONV_CB = 128


def _params(sem, vmem_limit_bytes=VMEM_LIMIT_BYTES):
    return pltpu.CompilerParams(dimension_semantics=sem, vmem_limit_bytes=vmem_limit_bytes)


def _norm_mod(x, g, shift, scale):
    y = x * lax.rsqrt(jnp.mean(x * x, axis=-1, keepdims=True) + EPS) * g
    return y * (1.0 + scale) + shift


def _silu(u):
    return u * jax.nn.sigmoid(u)


def _group_index_map(tm, n_prompt, rows_per_group, row0=0):
    def index_map(i, *_):
        row = row0 + i * tm
        grp = jnp.where(row < n_prompt, 0, 1 + (row - n_prompt) // rows_per_group)
        return (grp, 0, 0)
    return index_map


def _ada_kernel(c_ref, w_ref, b_ref, o_ref):
    a = _silu(c_ref[...]).astype(BF16)
    o_ref[...] = jnp.dot(a, w_ref[...].astype(BF16), preferred_element_type=F32) + b_ref[...]


def _ada(cond, w_ada, b_ada):
    depth, d, nout = w_ada.shape
    groups = cond.shape[0]
    g = -(-groups // SUBLANES) * SUBLANES
    cond = jnp.pad(cond, ((0, g - groups), (0, 0)))
    tn = min(ADA_TN, nout)
    out = pl.pallas_call(
        _ada_kernel,
        grid=(depth, nout // tn),
        in_specs=[
            pl.BlockSpec((g, d), lambda l, j: (0, 0)),
            pl.BlockSpec((None, d, tn), lambda l, j: (l, 0, j)),
            pl.BlockSpec((None, 1, tn), lambda l, j: (l, 0, j)),
        ],
        out_specs=pl.BlockSpec((None, g, tn), lambda l, j: (l, 0, j)),
        out_shape=jax.ShapeDtypeStruct((depth, g, nout), F32),
        compiler_params=_params(("arbitrary", "arbitrary")),
        name="ada",
    )(cond, w_ada, b_ada.reshape(depth, 1, nout))
    return out[:, :groups]


def _ffn_kernel(x_ref, mod_ref, g_ref, w13_ref, w2_ref, o_ref, h_ref):
    j = pl.program_id(1)
    tm = o_ref.shape[0]
    th = w2_ref.shape[0]
    gate = 0.5 * mod_ref[2:3, :]

    def gated_swiglu(h, w2):
        uv = jnp.dot(h, w13_ref[...], preferred_element_type=F32)
        a = (_silu(uv[:, :th]) * uv[:, th:]).astype(BF16)
        return gate * jnp.dot(a, w2, preferred_element_type=F32)

    @pl.when(j == 0)
    def _():
        sub = min(FFN_SUB0, tm)
        gs = g_ref[...] * (1.0 + mod_ref[1:2, :])
        shift = mod_ref[0:1, :]

        def norm(r0):
            x = x_ref[r0:r0 + sub, :]
            h = x * lax.rsqrt(jnp.mean(x * x, axis=-1, keepdims=True) + EPS) * gs + shift
            h = h.astype(BF16)
            h_ref[r0:r0 + sub, :] = h
            return x, h

        pending = norm(0)
        w2 = w2_ref[...].astype(BF16)
        for r0 in range(0, tm, sub):
            x, h = pending
            if r0 + sub < tm:
                pending = norm(r0 + sub)
            o_ref[r0:r0 + sub, :] = x + gated_swiglu(h, w2)

    @pl.when(j > 0)
    def _():
        sub = min(FFN_SUB, tm)
        w2 = w2_ref[...].astype(BF16)
        for r0 in range(0, tm, sub):
            o_ref[r0:r0 + sub, :] += gated_swiglu(h_ref[r0:r0 + sub, :], w2)


def _ffn_prep_kernel(w1_ref, w3_ref, o13_ref):
    th = w1_ref.shape[1]
    o13_ref[:, :th] = w1_ref[...].astype(BF16)
    o13_ref[:, th:] = w3_ref[...].astype(BF16)


def _ffn_weights(w1, w3):
    depth, halves, d, hid = w1.shape
    th = min(FFN_TH, hid)
    nj = hid // th
    return pl.pallas_call(
        _ffn_prep_kernel,
        grid=(depth, halves, nj),
        in_specs=[
            pl.BlockSpec((None, None, d, th), lambda l, s, j: (l, s, 0, j)),
            pl.BlockSpec((None, None, d, th), lambda l, s, j: (l, s, 0, j)),
        ],
        out_specs=pl.BlockSpec((None, None, None, d, 2 * th), lambda l, s, j: (l, s, j, 0, 0)),
        out_shape=jax.ShapeDtypeStruct((depth, halves, nj, d, 2 * th), BF16),
        compiler_params=_params(("arbitrary", "arbitrary", "arbitrary")),
        name="ffn_prep",
    )(w1, w3)


def _ffn(x, mod, g, w13, w2, layer, half, n_prompt, rows_per_group, row0=0, rows=None):
    d = x.shape[1]
    n = x.shape[0] if rows is None else rows
    nj, _, th2 = w13.shape[2:]
    th = th2 // 2
    tm = min(FFN_TM, n)
    tile0 = row0 // tm
    return pl.pallas_call(
        _ffn_kernel,
        grid=(n // tm, nj),
        in_specs=[
            pl.BlockSpec((tm, d), lambda i, j: (tile0 + i, 0)),
            pl.BlockSpec((None, 3, d), _group_index_map(tm, n_prompt, rows_per_group, row0)),
            pl.BlockSpec((1, d), lambda i, j: (0, 0)),
            pl.BlockSpec((None, None, None, d, th2), lambda i, j: (layer, half, j, 0, 0)),
            pl.BlockSpec((None, None, th, d), lambda i, j: (layer, half, j, 0)),
        ],
        out_specs=pl.BlockSpec((tm, d), lambda i, j: (i, 0)),
        out_shape=jax.ShapeDtypeStruct((n, d), F32),
        scratch_shapes=[pltpu.VMEM((tm, d), BF16)],
        compiler_params=_params(("arbitrary", "arbitrary"), FFN_VMEM_BYTES),
        name="ffn",
    )(x, mod, g, w13, w2)


def _tile_positions(tm, n_prompt, seq, n, dec_seq):
    starts = np.arange(0, n, tm)
    is_prompt = starts < n_prompt
    pos0 = np.where(is_prompt, starts % seq, (starts - n_prompt) % dec_seq)
    length = np.where(is_prompt, seq, dec_seq)
    return jnp.asarray(pos0, jnp.int32), jnp.asarray(length, jnp.int32)


def _halo_specs(tm, halo, d, n):
    per = tm // halo
    last = n // halo - 1
    prev = pl.BlockSpec((halo, d), lambda i, *_: (jnp.maximum(i * per - 1, 0), 0))
    nxt = pl.BlockSpec((halo, d), lambda i, *_: (jnp.minimum((i + 1) * per, last), 0))
    return prev, nxt


def _pool_kernel(pos_ref, len_ref, x_ref, xp_ref, xn_ref, mod_ref, g_ref, w_ref, sc_ref, o_ref, e_ref):
    i = pl.program_id(0)
    tm, d = x_ref.shape
    halo = xp_ref.shape[0]
    pos0 = pos_ref[i]
    seqlen = len_ref[i]
    g, shift, scale = g_ref[...], mod_ref[0:1, :], mod_ref[1:2, :]

    has_prev = (pos0 > 0).astype(F32)
    has_next = (pos0 + tm < seqlen).astype(F32)
    e_ref[0:halo, :] = _norm_mod(xp_ref[...], g, shift, scale) * has_prev
    e_ref[halo:halo + tm, :] = _norm_mod(x_ref[...], g, shift, scale)
    e_ref[halo + tm:, :] = _norm_mod(xn_ref[...], g, shift, scale) * has_next

    rows = e_ref.shape[0]
    pos = pos0 + lax.broadcasted_iota(jnp.int32, (tm, 1), 0)
    gw = d // len(POOL_WINDOWS)
    for gi, w in enumerate(POOL_WINDOWS):
        cols = slice(gi * gw, (gi + 1) * gw)
        e = e_ref[:, cols]
        s = pltpu.roll(e, 1, 0) + e
        half = 1
        while 2 * half < w:
            s = pltpu.roll(s, half, 0) + pltpu.roll(s, rows - half, 0)
            half *= 2
        s = s[halo:halo + tm, :]
        hi = jnp.minimum(pos + (w // 2 - 1), seqlen - 1)
        lo = jnp.maximum(pos - w // 2, 0)
        cnt = (hi - lo + 1).astype(F32)
        diff = (s / cnt - e[halo:halo + tm, :]).astype(BF16)
        y = jnp.dot(diff, w_ref[gi], preferred_element_type=F32) * sc_ref[:, cols]
        o_ref[:, cols] = x_ref[:, cols] + mod_ref[2:3, cols] * y


def _pool(x, mod, g, w_pool, pool_scale, pos0, seqlen, n_prompt, rows_per_group):
    n, d = x.shape
    tm = min(MIX_TM, n)
    prev, nxt = _halo_specs(tm, POOL_HALO, d, n)
    ng, gw, _ = w_pool.shape
    grid_spec = pltpu.PrefetchScalarGridSpec(
        num_scalar_prefetch=2,
        grid=(n // tm,),
        in_specs=[
            pl.BlockSpec((tm, d), lambda i, *_: (i, 0)),
            prev, nxt,
            pl.BlockSpec((None, 3, d), _group_index_map(tm, n_prompt, rows_per_group)),
            pl.BlockSpec((1, d), lambda i, *_: (0, 0)),
            pl.BlockSpec((ng, gw, gw), lambda i, *_: (0, 0, 0)),
            pl.BlockSpec((1, d), lambda i, *_: (0, 0)),
        ],
        out_specs=pl.BlockSpec((tm, d), lambda i, *_: (i, 0)),
        scratch_shapes=[pltpu.VMEM((tm + 2 * POOL_HALO, d), F32)],
    )
    return pl.pallas_call(
        _pool_kernel,
        grid_spec=grid_spec,
        out_shape=jax.ShapeDtypeStruct((n, d), F32),
        compiler_params=_params(("arbitrary",)),
        name="pool",
    )(pos0, seqlen, x, x, x, mod, g, w_pool, pool_scale)


def _rotate(main, swapped, cos, sin):
    return main * cos + swapped * sin


NT_DIMS = (((1,), (1,)), ((), ()))


def _mla_pre_kernel(x_ref, mod_ref, g_ref, cos_ref, sin_ref, wdq_ref, qn_ref, wq_ref, wqs_ref,
                    wkv_ref, wpet_ref, kvn_ref, gq_ref, gqs_ref, q_ref, ckv_ref, kpe_ref, pet_ref):
    h = _norm_mod(x_ref[...], g_ref[...], mod_ref[0:1, :], mod_ref[1:2, :]).astype(BF16)
    cos, sin = cos_ref[...], sin_ref[...]

    cq = jnp.dot(h, wdq_ref[...], preferred_element_type=F32)
    cq = cq * lax.rsqrt(jnp.mean(cq * cq, axis=-1, keepdims=True) + EPS) * qn_ref[...]
    cq = cq.astype(BF16)
    q = jnp.dot(cq, wq_ref[...], preferred_element_type=F32)
    qs = jnp.dot(cq, wqs_ref[...], preferred_element_type=F32)
    gq, gqs = gq_ref[...], gqs_ref[...]
    for hd in range(q_ref.shape[0]):
        qh = q[:, hd * HEAD_PAD:(hd + 1) * HEAD_PAD]
        rinv = lax.rsqrt(jnp.sum(qh * qh, axis=-1, keepdims=True) / QK_HEAD + EPS)
        qh = qh * rinv * gq
        qsw = qs[:, hd * ROPE_PAD:(hd + 1) * ROPE_PAD] * rinv * gqs
        rope = _rotate(qh[:, :ROPE_PAD], qsw, cos, sin)
        q_ref[hd] = jnp.concatenate([rope, qh[:, ROPE_PAD:]], axis=-1).astype(BF16)

    kv = jnp.dot(h, wkv_ref[...], preferred_element_type=F32)
    lora = ckv_ref.shape[1]
    c = kv[:, :lora]
    ckv_ref[...] = c * lax.rsqrt(jnp.mean(c * c, axis=-1, keepdims=True) + EPS) * kvn_ref[...]
    kpe_ref[...] = kv[:, lora:]
    pet_ref[...] = lax.dot_general(wpet_ref[...], h, NT_DIMS, preferred_element_type=F32)


def _mla_pre(x, mod, g, cos, sin, wdq, qn, wq, wqs, wkv, wpet, kvn, gq, gqs, n_prompt, rows_per_group):
    n, d = x.shape
    tm = min(MIX_TM, n)
    qlora = wdq.shape[1]
    lora = kvn.shape[1]
    heads = wq.shape[1] // HEAD_PAD
    const = lambda shape: pl.BlockSpec(shape, lambda i: (0,) * len(shape))
    row = lambda width: pl.BlockSpec((tm, width), lambda i: (i, 0))
    return pl.pallas_call(
        _mla_pre_kernel,
        grid=(n // tm,),
        in_specs=[
            row(d),
            pl.BlockSpec((None, 3, d), _group_index_map(tm, n_prompt, rows_per_group)),
            const((1, d)), row(ROPE_PAD), row(ROPE_PAD),
            const((d, qlora)), const((1, qlora)),
            const((qlora, heads * HEAD_PAD)), const((qlora, heads * ROPE_PAD)),
            const((d, lora + ROPE_PAD)), const((2 * ROPE_PAD, d)), const((1, lora)),
            const((1, HEAD_PAD)), const((1, ROPE_PAD)),
        ],
        out_specs=[
            pl.BlockSpec((heads, tm, HEAD_PAD), lambda i: (0, i, 0)),
            row(lora), row(ROPE_PAD),
            pl.BlockSpec((2 * ROPE_PAD, tm), lambda i: (0, i)),
        ],
        out_shape=[
            jax.ShapeDtypeStruct((heads, n, HEAD_PAD), BF16),
            jax.ShapeDtypeStruct((n, lora), F32),
            jax.ShapeDtypeStruct((n, ROPE_PAD), F32),
            jax.ShapeDtypeStruct((2 * ROPE_PAD, n), F32),
        ],
        compiler_params=_params(("arbitrary",)),
        name="mla_pre",
    )(x, mod, g, cos, sin, wdq, qn, wq, wqs, wkv, wpet, kvn, gq, gqs)


def _mla_expand_kernel(ckv_ref, pet_ref, cos_ref, sin_ref, wkt_ref, wv_ref, gk_ref, gks_ref,
                       kt_ref, v_ref):
    c = ckv_ref[...].astype(BF16)
    knt = lax.dot_general(wkt_ref[...], c, NT_DIMS, preferred_element_type=F32)
    v = jnp.dot(c, wv_ref[...], preferred_element_type=F32)
    gk, gks = gk_ref[...], gks_ref[...]
    pet, pst = pet_ref[:ROPE_PAD, :], pet_ref[ROPE_PAD:, :]
    rope = _rotate(pet * gk[:ROPE_PAD], pst * gks, cos_ref[...], sin_ref[...])
    pe_sq = jnp.sum(pet * pet, axis=0, keepdims=True)
    for hd in range(kt_ref.shape[0]):
        kn = knt[hd * QK_NOPE:(hd + 1) * QK_NOPE, :]
        rinv = lax.rsqrt((jnp.sum(kn * kn, axis=0, keepdims=True) + pe_sq) / QK_HEAD + EPS)
        kt_ref[hd] = jnp.concatenate([rope * rinv, kn * rinv * gk[ROPE_PAD:]], axis=0).astype(BF16)
        v_ref[hd] = v[:, hd * V_HEAD:(hd + 1) * V_HEAD].astype(BF16)


def _mla_expand(ckv, pet, cos_t, sin_t, wkt, wv, gk, gks):
    n, lora = ckv.shape
    tm = min(MIX_TM, n)
    heads = wv.shape[1] // V_HEAD
    const = lambda shape: pl.BlockSpec(shape, lambda i: (0,) * len(shape))
    col = lambda height: pl.BlockSpec((height, tm), lambda i: (0, i))
    return pl.pallas_call(
        _mla_expand_kernel,
        grid=(n // tm,),
        in_specs=[pl.BlockSpec((tm, lora), lambda i: (i, 0)), col(2 * ROPE_PAD), col(ROPE_PAD), col(ROPE_PAD),
                  const(wkt.shape), const(wv.shape), const((HEAD_PAD, 1)), const((ROPE_PAD, 1))],
        out_specs=[pl.BlockSpec((heads, HEAD_PAD, tm), lambda i: (0, 0, i)),
                   pl.BlockSpec((heads, tm, V_HEAD), lambda i: (0, i, 0))],
        out_shape=[jax.ShapeDtypeStruct((heads, HEAD_PAD, n), BF16),
                   jax.ShapeDtypeStruct((heads, n, V_HEAD), BF16)],
        compiler_params=_params(("arbitrary",)),
        name="mla_expand",
    )(ckv, pet, cos_t, sin_t, wkt, wv, gk, gks)


def _attn_kernel(*refs, n_parts, scale):
    for hd in range(refs[0].shape[0]):
        _attn_head(refs[0].at[hd], [(refs[1 + 2 * p].at[hd], refs[2 + 2 * p].at[hd]) for p in range(n_parts)],
                   refs[-1].at[:, hd * V_HEAD:(hd + 1) * V_HEAD], scale)


def _attn_head(q_ref, parts, o_ref, scale):
    c = scale * np.log2(np.e)
    sub = min(ATT_SUB, q_ref.shape[0])
    chunks = []
    for kt_ref, v_ref in parts:
        tk = min(ATT_TK, v_ref.shape[0])
        chunks += [(kt_ref, v_ref, k0, tk) for k0 in range(0, v_ref.shape[0], tk)]

    def score(r0, chunk):
        kt_ref, _, k0, tk = chunk
        return jnp.dot(q_ref[r0:r0 + sub, :], kt_ref[:, k0:k0 + tk], preferred_element_type=F32)

    tq = q_ref.shape[0]
    s = [score(0, chunk) for chunk in chunks]
    for r0 in range(0, tq, sub):
        m = functools.reduce(jnp.maximum, [jnp.max(sp, axis=-1, keepdims=True) for sp in s])
        s_next = []
        l, acc = 0.0, 0.0
        for sp, chunk in zip(s, chunks):
            if r0 + sub < tq:
                s_next.append(score(r0 + sub, chunk))
            _, v_ref, k0, tk = chunk
            e = jnp.exp2((sp - m) * c)
            l = l + jnp.sum(e, axis=-1, keepdims=True)
            acc = acc + jnp.dot(e.astype(BF16), v_ref[k0:k0 + tk, :], preferred_element_type=F32)
        o_ref[r0:r0 + sub, :] = (acc / l).astype(BF16)
        s = s_next


def _attention(q, parts, q_row0, n_batch, t_q, out_rows):
    heads = q.shape[0]
    tq = min(ATT_TQ, t_q)
    nq = t_q // tq
    qb0 = q_row0 // tq
    keys = sum(t_k for _, _, _, t_k in parts)
    hb = max(1, min(heads, (ATT_TQ * ATT_KEYS) // (tq * keys)))
    while heads % hb:
        hb -= 1
    in_specs = [pl.BlockSpec((hb, tq, HEAD_PAD), lambda b, h, i: (h, qb0 + b * nq + i, 0))]
    args = [q]
    for kt, v, row0, t_k in parts:
        kb0 = row0 // t_k
        in_specs.append(pl.BlockSpec((hb, HEAD_PAD, t_k), lambda b, h, i, kb0=kb0: (h, 0, kb0 + b)))
        in_specs.append(pl.BlockSpec((hb, t_k, V_HEAD), lambda b, h, i, kb0=kb0: (h, kb0 + b, 0)))
        args += [kt, v]
    return pl.pallas_call(
        functools.partial(_attn_kernel, n_parts=len(parts), scale=QK_HEAD ** -0.5),
        grid=(n_batch, heads // hb, nq),
        in_specs=in_specs,
        out_specs=pl.BlockSpec((tq, hb * V_HEAD), lambda b, h, i: (b * nq + i, h)),
        out_shape=jax.ShapeDtypeStruct((out_rows, heads * V_HEAD), BF16),
        compiler_params=_params(("arbitrary", "arbitrary", "arbitrary")),
        name="attn",
    )(*args)


def _proj_res_kernel(x_ref, a_ref, mod_ref, w_ref, o_ref):
    y = jnp.dot(a_ref[...], w_ref[...], preferred_element_type=F32)
    o_ref[...] = x_ref[...] + mod_ref[2:3, :] * y


def _proj_res(x, a, mod, w, n_prompt, rows_per_group):
    n, d = x.shape
    tm = min(MIX_TM, n)
    return pl.pallas_call(
        _proj_res_kernel,
        grid=(n // tm,),
        in_specs=[
            pl.BlockSpec((tm, d), lambda i: (i, 0)),
            pl.BlockSpec((tm, a.shape[1]), lambda i: (i, 0)),
            pl.BlockSpec((None, 3, d), _group_index_map(tm, n_prompt, rows_per_group)),
            pl.BlockSpec(w.shape, lambda i: (0, 0)),
        ],
        out_specs=pl.BlockSpec((tm, d), lambda i: (i, 0)),
        out_shape=jax.ShapeDtypeStruct((n, d), F32),
        compiler_params=_params(("arbitrary",)),
        name="proj_res",
    )(x, a, mod, w)


def _glu_kernel(x_ref, mod_ref, g_ref, wa_ref, wb_ref, ba_ref, bb_ref, o_ref, h_ref):
    j = pl.program_id(1)
    tm = x_ref.shape[0]

    def glu(h):
        a = jnp.dot(h, wa_ref[...], preferred_element_type=F32) + ba_ref[...]
        b = jnp.dot(h, wb_ref[...], preferred_element_type=F32) + bb_ref[...]
        return a * jax.nn.sigmoid(b)

    @pl.when(j == 0)
    def _():
        sub = min(FFN_SUB0, tm)

        def norm(r0):
            h = _norm_mod(x_ref[r0:r0 + sub, :], g_ref[...], mod_ref[0:1, :], mod_ref[1:2, :]).astype(BF16)
            h_ref[r0:r0 + sub, :] = h
            return h

        pending = norm(0)
        for r0 in range(0, tm, sub):
            h = pending
            if r0 + sub < tm:
                pending = norm(r0 + sub)
            o_ref[r0:r0 + sub, :] = glu(h)

    @pl.when(j > 0)
    def _():
        o_ref[...] = glu(h_ref[...])


def _glu(x, mod, g, w1, b1, n_prompt, rows_per_group):
    n, d = x.shape
    tm, tn = min(GLU_TM, n), min(CONV_TN, d)
    nj = d // tn
    return pl.pallas_call(
        _glu_kernel,
        grid=(n // tm, nj),
        in_specs=[
            pl.BlockSpec((tm, d), lambda i, j: (i, 0)),
            pl.BlockSpec((None, 3, d), _group_index_map(tm, n_prompt, rows_per_group)),
            pl.BlockSpec((1, d), lambda i, j: (0, 0)),
            pl.BlockSpec((d, tn), lambda i, j: (0, j)),
            pl.BlockSpec((d, tn), lambda i, j: (0, nj + j)),
            pl.BlockSpec((1, tn), lambda i, j: (0, j)),
            pl.BlockSpec((1, tn), lambda i, j: (0, nj + j)),
        ],
        out_specs=pl.BlockSpec((tm, tn), lambda i, j: (i, j)),
        out_shape=jax.ShapeDtypeStruct((n, d), F32),
        scratch_shapes=[pltpu.VMEM((tm, d), BF16)],
        compiler_params=_params(("arbitrary", "arbitrary")),
        name="glu",
    )(x, mod, g, w1, w1, b1, b1)


def _conv_kernel(pos_ref, len_ref, x_ref, u_ref, up_ref, un_ref, mod_ref, dw_ref, dwb_ref,
                 lg_ref, lb_ref, w_ref, b_ref, o_ref, e_ref, c_ref):
    i = pl.program_id(0)
    tm, d = x_ref.shape
    halo = up_ref.shape[0]
    width = dw_ref.shape[0]
    pos0 = pos_ref[i]
    seqlen = len_ref[i]

    has_prev = (pos0 > 0).astype(F32)
    has_next = (pos0 + tm < seqlen).astype(F32)
    e_ref[0:halo, :] = up_ref[...] * has_prev
    e_ref[halo:halo + tm, :] = u_ref[...]
    e_ref[halo + tm:, :] = un_ref[...] * has_next

    rb, cb = min(CONV_RB, tm), min(CONV_CB, d)
    first = halo - width // 2
    for c0 in range(0, d, cb):
        cols = slice(c0, c0 + cb)
        for r0 in range(0, tm, rb):
            acc = jnp.zeros((rb, cb), F32) + dwb_ref[:, cols]
            for sp in range(SUBLANES):
                g = None
                for k in range(width):
                    if (first + k) % SUBLANES != sp:
                        continue
                    base = r0 + first + k - sp
                    term = e_ref[base:base + rb + SUBLANES, cols] * dw_ref[k:k + 1, cols]
                    g = term if g is None else g + term
                if g is not None:
                    acc = acc + g[sp:sp + rb, :]
            c_ref[r0:r0 + rb, cols] = acc

    c = c_ref[...]
    mu = jnp.mean(c, axis=-1, keepdims=True)
    var = jnp.mean(jnp.square(c - mu), axis=-1, keepdims=True)
    y = (c - mu) * lax.rsqrt(var + EPS) * lg_ref[...] + lb_ref[...]
    y = _silu(y).astype(BF16)
    y = jnp.dot(y, w_ref[...], preferred_element_type=F32) + b_ref[...]
    o_ref[...] = x_ref[...] + mod_ref[2:3, :] * y


def _conv(x, u, mod, dw, dw_b, ln_g, ln_b, w2, b2, pos0, seqlen, n_prompt, rows_per_group):
    n, d = x.shape
    tm = min(MIX_TM, n)
    prev, nxt = _halo_specs(tm, CONV_HALO, d, n)
    const = lambda shape: pl.BlockSpec(shape, lambda i, *_: (0,) * len(shape))
    grid_spec = pltpu.PrefetchScalarGridSpec(
        num_scalar_prefetch=2,
        grid=(n // tm,),
        in_specs=[
            pl.BlockSpec((tm, d), lambda i, *_: (i, 0)),
            pl.BlockSpec((tm, d), lambda i, *_: (i, 0)),
            prev, nxt,
            pl.BlockSpec((None, 3, d), _group_index_map(tm, n_prompt, rows_per_group)),
            const(dw.shape), const((1, d)), const((1, d)), const((1, d)),
            const(w2.shape), const((1, d)),
        ],
        out_specs=pl.BlockSpec((tm, d), lambda i, *_: (i, 0)),
        scratch_shapes=[pltpu.VMEM((tm + 2 * CONV_HALO, d), F32), pltpu.VMEM((tm, d), F32)],
    )
    return pl.pallas_call(
        _conv_kernel,
        grid_spec=grid_spec,
        out_shape=jax.ShapeDtypeStruct((n, d), F32),
        compiler_params=_params(("arbitrary",)),
        name="conv",
    )(pos0, seqlen, x, u, u, u, mod, dw, dw_b, ln_g, ln_b, w2, b2)


def _rope_swap(a):
    q = QK_ROPE // 4
    lead = a.shape[:-1]
    return a.reshape(lead + (2, 2, q))[..., ::-1, :].reshape(lead + (QK_ROPE,))


def _pad_rope(a):
    return jnp.pad(a, [(0, 0)] * (a.ndim - 1) + [(0, ROPE_PAD - QK_ROPE)])


def _rope_tables(t):
    rows = t // GRID_W
    row = jnp.repeat(jnp.arange(rows), GRID_W).astype(F32)
    col = jnp.tile(jnp.arange(GRID_W), rows).astype(F32)
    half = QK_ROPE // 2
    inv_freq = ROPE_BASE ** (-jnp.arange(0, half, 2, dtype=F32) / half)
    ang_r, ang_c = row[:, None] * inv_freq, col[:, None] * inv_freq
    cos = jnp.concatenate([jnp.cos(ang_r)] * 2 + [jnp.cos(ang_c)] * 2, axis=-1)
    sin = jnp.concatenate([-jnp.sin(ang_r), jnp.sin(ang_r), -jnp.sin(ang_c), jnp.sin(ang_c)], axis=-1)
    pad = ROPE_PAD - QK_ROPE
    cos = jnp.concatenate([cos, jnp.ones((t, pad), F32)], axis=-1)
    sin = jnp.concatenate([sin, jnp.zeros((t, pad), F32)], axis=-1)
    return cos, sin


def _head_layout(w, heads):
    rows = w.shape[0]
    w = w.reshape(rows, heads, QK_HEAD)
    nope, rope = w[..., :QK_NOPE], w[..., QK_NOPE:]
    main = jnp.concatenate([_pad_rope(rope), nope], axis=-1).reshape(rows, heads * HEAD_PAD)
    swapped = _pad_rope(_rope_swap(rope)).reshape(rows, heads * ROPE_PAD)
    return main, swapped


def _gain_layout(gain):
    nope, rope = gain[:QK_NOPE], gain[QK_NOPE:]
    main = jnp.concatenate([_pad_rope(rope), nope])[None, :]
    return main, _pad_rope(_rope_swap(rope))[None, :]


def kernel(x_prompt, x_sample, cache_ckv, cache_kpe, c, c_ctx, norm_g, w_ada, b_ada, ffn_w1, ffn_w3, ffn_w2, pool_w, pool_scale, mla_w_dq, mla_q_norm, mla_w_uq, mla_w_dkv, mla_kv_norm, mla_w_ukv, mla_w_o, mla_q_gain, mla_k_gain, conv_w1, conv_b1, conv_dw, conv_dw_b, conv_ln_g, conv_ln_b, conv_w2, conv_b2):
    batch, seq, d = x_prompt.shape
    dec_batch, dec_seq, _ = x_sample.shape
    depth = norm_g.shape[0]
    past_len = cache_ckv.shape[2]
    lora = mla_kv_norm.shape[1]
    heads = MLA_HEADS
    n_prompt = batch * seq
    n_sample = dec_batch * dec_seq
    n = n_prompt + n_sample
    rows_per_group = dec_seq
    for tm in (FFN_TM, MIX_TM):
        assert n_prompt % tm == 0 and dec_seq % tm == 0
    assert seq % MIX_TM == 0 and n_prompt % dec_seq == 0 and dec_seq % GRID_W == 0
    state_ckv, state_kpe = [], []

    x = jnp.concatenate([x_prompt.reshape(n_prompt, d), x_sample.reshape(n_sample, d)], axis=0)
    cond = jnp.concatenate([c_ctx[None, :], c], axis=0)
    groups = cond.shape[0]
    mod_all = _ada(cond, w_ada, b_ada).reshape(depth, groups, N_MOD // 3, 3, d)

    pos0, seqlen = _tile_positions(min(MIX_TM, n), n_prompt, seq, n, dec_seq)

    w13, w2 = _ffn_weights(ffn_w1, ffn_w3), ffn_w2

    for i in range(depth):
        mods = [mod_all[i, :, s] for s in range(3)]
        ng = lambda s: norm_g[i, s][None, :]
        x = _ffn(x, mods[0], ng(0), w13, w2, i, 0, n_prompt, rows_per_group)
        kind, j = i % N_MIXERS, i // N_MIXERS
        if kind == 0:
            x = _pool(x, mods[1], ng(1), pool_w[j].astype(BF16), pool_scale[j][None, :], pos0, seqlen,
                      n_prompt, rows_per_group)
        elif kind == 1:
            cos_t, sin_t = _rope_tables(dec_seq)
            cos = jnp.concatenate([jnp.ones((n_prompt, ROPE_PAD), F32)] + [cos_t] * dec_batch, axis=0)
            sin = jnp.concatenate([jnp.zeros((n_prompt, ROPE_PAD), F32)] + [sin_t] * dec_batch, axis=0)
            wq, wqs = _head_layout(mla_w_uq[j], heads)
            wd = mla_w_dkv[j]
            wpe = _pad_rope(wd[:, lora:])
            wkv = jnp.concatenate([wd[:, :lora], wpe], axis=-1)
            wpet = jnp.concatenate([wpe, _pad_rope(_rope_swap(wd[:, lora:]))], axis=-1).T
            gq, gqs = _gain_layout(mla_q_gain[j])
            gk, gks = _gain_layout(mla_k_gain[j])
            q, ckv, kpe, pet = _mla_pre(
                x, mods[1], ng(1), cos, sin, mla_w_dq[j].astype(BF16), mla_q_norm[j][None, :],
                wq.astype(BF16), wqs.astype(BF16), wkv.astype(BF16), wpet.astype(BF16),
                mla_kv_norm[j][None, :], gq, gqs, n_prompt, rows_per_group)
            state_ckv.append(ckv[:n_prompt].reshape(batch, seq, lora))
            state_kpe.append(kpe[:n_prompt, :QK_ROPE].reshape(batch, seq, QK_ROPE))
            w_ukv = mla_w_ukv[j].reshape(lora, heads, QK_NOPE + V_HEAD)
            wkt = w_ukv[..., :QK_NOPE].reshape(lora, heads * QK_NOPE).T.astype(BF16)
            wv = w_ukv[..., QK_NOPE:].reshape(lora, heads * V_HEAD).astype(BF16)
            k_tok, v_tok = _mla_expand(ckv, pet, cos.T, sin.T, wkt, wv, gk.T, gks.T)
            n_cache = dec_batch * past_len
            pet_ctx = jnp.concatenate([_pad_rope(cache_kpe[:, j].reshape(n_cache, QK_ROPE)).T,
                                       jnp.zeros((ROPE_PAD, n_cache), F32)], axis=0)
            k_ctx, v_ctx = _mla_expand(
                cache_ckv[:, j].reshape(n_cache, lora), pet_ctx,
                jnp.ones((ROPE_PAD, n_cache), F32), jnp.zeros((ROPE_PAD, n_cache), F32),
                wkt, wv, gk.T, gks.T)
            o_prompt = _attention(q, [(k_tok, v_tok, 0, seq)], 0, batch, seq, n_prompt)
            o_sample = _attention(q, [(k_ctx, v_ctx, 0, past_len), (k_tok, v_tok, n_prompt, dec_seq)],
                                  n_prompt, dec_batch, dec_seq, n_sample)
            o = jnp.concatenate([o_prompt, o_sample], axis=0)
            x = _proj_res(x, o, mods[1], mla_w_o[j].astype(BF16), n_prompt, rows_per_group)
        else:
            u = _glu(x, mods[1], ng(1), conv_w1[j].astype(BF16), conv_b1[j][None, :], n_prompt, rows_per_group)
            x = _conv(x, u, mods[1], conv_dw[j], conv_dw_b[j][None, :], conv_ln_g[j][None, :],
                      conv_ln_b[j][None, :], conv_w2[j].astype(BF16), conv_b2[j][None, :],
                      pos0, seqlen, n_prompt, rows_per_group)
        if i + 1 < depth:
            x = _ffn(x, mods[2], ng(2), w13, w2, i, 1, n_prompt, rows_per_group)
        else:
            last = (mods[2], ng(2), w13, w2, i, 1, n_prompt, rows_per_group)
            y_prompt = _ffn(x, *last, row0=0, rows=n_prompt).reshape(batch, seq, d)
            y_sample = _ffn(x, *last, row0=n_prompt, rows=n_sample).reshape(dec_batch, dec_seq, d)

    return (y_prompt, y_sample, jnp.stack(state_ckv, axis=1), jnp.stack(state_kpe, axis=1))
```
